```python
import jax, jax.numpy as jnp
from jax import lax
import numpy as np

D_MODEL = 1024
BATCH = 2
SEQ = 8192
DEPTH = 4
DEC_BATCH = 128
DEC_SEQ = 8
PAST_LEN = 8192
PAGE_SIZE = 128

N_MIXERS = 3
MIXER_OF_LAYER = tuple(i % N_MIXERS for i in range(DEPTH))
MIXER_ORDINAL = tuple(MIXER_OF_LAYER[:i].count(MIXER_OF_LAYER[i]) for i in range(DEPTH))
N_MLA = MIXER_OF_LAYER.count(0)
N_GMLP = MIXER_OF_LAYER.count(1)
N_RGLRU = MIXER_OF_LAYER.count(2)
N_DENSE = (DEPTH + 1) // 2
N_MOE = DEPTH // 2

EPS = 1e-6
N_HEADS = 8
NOPE_DIM = 128
ROPE_DIM = 64
V_DIM = 128
Q_RANK = 384
KV_RANK = 256
ROPE_THETA = 10000.0
Q_BLOCK = 128
SM_SCALE = (NOPE_DIM + ROPE_DIM) ** -0.5
CHUNK = 128
E_B = 2048
G_B = 8
DG_B = E_B // G_B
E_C = 1536
NB_C = 8
DB_C = E_C // NB_C
CONV_W = 4
C_RG = 8.0
D_FF = 2816
N_EXPERTS = 8
TOP_K = 2
D_FF_E = 3584

kernel_name = 'hybrid_mla_gmlp_rglru_decode_step'


def rmsnorm(x, g):
    xf = x.astype(jnp.float32)
    y = xf * lax.rsqrt(jnp.mean(xf * xf, axis=-1, keepdims=True) + EPS)
    return (y * g.astype(jnp.float32)).astype(x.dtype)


def rope(x, pos):
    half = ROPE_DIM // 2
    inv_freq = ROPE_THETA ** (-jnp.arange(half, dtype=jnp.float32) / half)
    ang = pos.astype(jnp.float32)[:, None] * inv_freq[None, :]
    cos = jnp.cos(ang)[None, :, None, :]
    sin = jnp.sin(ang)[None, :, None, :]
    xf = x.astype(jnp.float32)
    x1, x2 = xf[..., :half], xf[..., half:]
    return jnp.concatenate([x1 * cos - x2 * sin, x2 * cos + x1 * sin], axis=-1).astype(x.dtype)


def mla_project(h, pos, wq_a, g_qa, wq_b, wkv_a, g_kva, w_uk, g_qn, g_qr, g_kn, g_kr):
    c_q = rmsnorm(h @ wq_a, g_qa)
    q = jnp.einsum('bsr,rhd->bshd', c_q, wq_b)
    q_nope = rmsnorm(q[..., :NOPE_DIM], g_qn)
    q_pe = rope(rmsnorm(q[..., NOPE_DIM:], g_qr), pos)
    kv = h @ wkv_a
    c_kv = rmsnorm(kv[..., :KV_RANK], g_kva)
    k_pe = rope(rmsnorm(kv[..., KV_RANK:], g_kr)[:, :, None, :], pos)[:, :, 0, :]
    k_nope = jnp.einsum('bsr,rhd->bshd', c_kv, w_uk).astype(jnp.float32)
    k_scale = lax.rsqrt(jnp.mean(k_nope * k_nope, axis=-1) + EPS).astype(h.dtype)
    q_lat = jnp.einsum('bshd,rhd->bshr', q_nope * g_kn, w_uk)
    kv_row = jnp.concatenate([c_kv, k_pe], axis=-1)
    return q_lat, q_pe, kv_row, k_scale


def mla_attend(q_lat, q_pe, kv_rows, k_scale, mask):
    c = kv_rows[..., :KV_RANK]
    kpe = kv_rows[..., KV_RANK:]
    s_nope = jnp.einsum('bshr,btr->bhst', q_lat, c, preferred_element_type=jnp.float32)
    s_pe = jnp.einsum('bshp,btp->bhst', q_pe, kpe, preferred_element_type=jnp.float32)
    ks = jnp.swapaxes(k_scale, 1, 2).astype(jnp.float32)[:, :, None, :]
    s = (s_nope * ks + s_pe) * SM_SCALE
    s = jnp.where(mask, s, jnp.finfo(jnp.float32).min)
    p = jax.nn.softmax(s, axis=-1)
    return jnp.einsum('bhst,btr->bshr', p.astype(c.dtype), c)


def mla_prompt_attend(q_lat, q_pe, kv_rows, k_scale):
    b, s = q_lat.shape[0], q_lat.shape[1]
    nb = s // Q_BLOCK
    qb = jnp.swapaxes(q_lat.reshape(b, nb, Q_BLOCK, N_HEADS, KV_RANK), 0, 1)
    pb = jnp.swapaxes(q_pe.reshape(b, nb, Q_BLOCK, N_HEADS, ROPE_DIM), 0, 1)
    k_pos = jnp.arange(s)

    def one_block(args):
        i, ql, qp = args
        q_pos = i * Q_BLOCK + jnp.arange(Q_BLOCK)
        mask = q_pos[:, None] >= k_pos[None, :]
        return mla_attend(ql, qp, kv_rows, k_scale, mask)

    o = lax.map(one_block, (jnp.arange(nb), qb, pb))
    return jnp.swapaxes(o, 0, 1).reshape(b, s, N_HEADS, KV_RANK)


def mla_output(o_lat, w_uv, wo):
    o = jnp.einsum('bshr,rhd->bshd', o_lat, w_uv)
    return jnp.einsum('bshd,hde->bse', o, wo)


def gmlp_mix(h, w_in, g_v, w_s, b_s, w_out):
    b, s = h.shape[0], h.shape[1]
    L = min(s, CHUNK)
    z = jax.nn.gelu(h @ w_in)
    u, v = z[..., :E_B], z[..., E_B:]
    v = rmsnorm(v, g_v)
    tri = jnp.tril(jnp.ones((L, L), dtype=bool))
    ws = jnp.where(tri[None], w_s[:, :L, :L], 0)
    vc = v.reshape(b, s // L, L, G_B, DG_B)
    mixed = jnp.einsum('gts,bcsgd->bctgd', ws, vc) + b_s[:, :L].T[None, None, :, :, None]
    y = u * mixed.reshape(b, s, E_B)
    return y @ w_out, v


def rglru_mix(h, h0, conv_buf, w_gate, w_x, conv_w, conv_b, w_a, b_a, w_i, b_i, lam, w_out):
    b, s = h.shape[0], h.shape[1]
    gate = jax.nn.gelu(h @ w_gate)
    xb = h @ w_x
    xpad = jnp.concatenate([conv_buf.astype(xb.dtype), xb], axis=1)
    xc = conv_b + sum(xpad[:, k:k + s] * conv_w[k] for k in range(CONV_W))
    new_buf = xpad[:, s:]
    xblk = xc.reshape(b, s, NB_C, DB_C)
    r = jax.nn.sigmoid(jnp.einsum('bsnd,nde->bsne', xblk, w_a).reshape(b, s, E_C) + b_a)
    i = jax.nn.sigmoid(jnp.einsum('bsnd,nde->bsne', xblk, w_i).reshape(b, s, E_C) + b_i)
    log_a = -C_RG * r.astype(jnp.float32) * jax.nn.softplus(-lam.astype(jnp.float32))
    a = jnp.exp(log_a)
    mult = jnp.sqrt(-jnp.expm1(2.0 * log_a))
    u = mult * i.astype(jnp.float32) * xc.astype(jnp.float32)
    u = u.at[:, 0].add(a[:, 0] * h0.astype(jnp.float32))

    def combine(e1, e2):
        a1, b1 = e1
        a2, b2 = e2
        return a1 * a2, a2 * b1 + b2

    _, hs = lax.associative_scan(combine, (a, u), axis=1)
    y = (hs.astype(h.dtype) * gate) @ w_out
    return y, hs[:, -1].astype(h.dtype), new_buf


def swiglu(h, w1, w3, w2):
    return (jax.nn.silu(h @ w1) * (h @ w3)) @ w2


def moe_ffn(h, router, w1, w3, w2):
    logits = (h @ router).astype(jnp.float32)
    top_v, top_i = lax.top_k(logits, TOP_K)
    gates = jax.nn.softmax(top_v, axis=-1)
    combine = jnp.sum(jax.nn.one_hot(top_i, N_EXPERTS, dtype=jnp.float32) * gates[..., None], axis=-2)
    out = jnp.zeros_like(h)
    for e in range(N_EXPERTS):
        out = out + combine[..., e:e + 1].astype(h.dtype) * swiglu(h, w1[e], w3[e], w2[e])
    return out


def setup_inputs(seed: int = 0) -> dict:
    key = jax.random.key(seed)
    keys = iter(jax.random.split(key, 64))
    f32 = jnp.float32

    def nrm(shape, scale=1.0):
        return jax.random.normal(next(keys), shape, f32) * scale

    def gain(shape):
        return 1.0 + 0.05 * nrm(shape)

    d = D_MODEL
    n_pages = PAST_LEN // PAGE_SIZE
    n_used = DEC_BATCH * n_pages
    n_pool = n_used + n_used // 4
    page_table = jax.random.permutation(next(keys), n_pool)[:n_used].reshape(DEC_BATCH, n_pages).astype(jnp.int32)
    a_init = jax.random.uniform(next(keys), (N_RGLRU, E_C), f32, 0.9, 0.999)
    return {
        'x_prompt': nrm((BATCH, SEQ, d)),
        'x_sample': nrm((DEC_BATCH, DEC_SEQ, d)),
        'cache_mla_kv': nrm((N_MLA, n_pool, PAGE_SIZE, KV_RANK + ROPE_DIM)),
        'cache_mla_kscale': jax.random.uniform(next(keys), (N_MLA, n_pool, PAGE_SIZE, N_HEADS), f32, 0.5, 1.5),
        'state_rglru_h': nrm((N_RGLRU, DEC_BATCH, E_C), 0.5),
        'state_rglru_conv': nrm((N_RGLRU, DEC_BATCH, CONV_W - 1, E_C)),
        'page_table': page_table,
        'g_mix': gain((DEPTH, d)),
        'g_ffn': gain((DEPTH, d)),
        'mla_wq_a': nrm((N_MLA, d, Q_RANK), d ** -0.5),
        'mla_g_qa': gain((N_MLA, Q_RANK)),
        'mla_wq_b': nrm((N_MLA, Q_RANK, N_HEADS, NOPE_DIM + ROPE_DIM), Q_RANK ** -0.5),
        'mla_wkv_a': nrm((N_MLA, d, KV_RANK + ROPE_DIM), d ** -0.5),
        'mla_g_kva': gain((N_MLA, KV_RANK)),
        'mla_w_uk': nrm((N_MLA, KV_RANK, N_HEADS, NOPE_DIM), KV_RANK ** -0.5),
        'mla_w_uv': nrm((N_MLA, KV_RANK, N_HEADS, V_DIM), KV_RANK ** -0.5),
        'mla_wo': nrm((N_MLA, N_HEADS, V_DIM, d), (N_HEADS * V_DIM) ** -0.5),
        'mla_g_qn': gain((N_MLA, NOPE_DIM)),
        'mla_g_qr': gain((N_MLA, ROPE_DIM)),
        'mla_g_kn': gain((N_MLA, NOPE_DIM)),
        'mla_g_kr': gain((N_MLA, ROPE_DIM)),
        'gm_w_in': nrm((N_GMLP, d, 2 * E_B), d ** -0.5),
        'gm_g_v': gain((N_GMLP, E_B)),
        'gm_w_s': nrm((N_GMLP, G_B, CHUNK, CHUNK), CHUNK ** -0.5),
        'gm_b_s': 1.0 + 0.1 * nrm((N_GMLP, G_B, CHUNK)),
        'gm_w_out': nrm((N_GMLP, E_B, d), E_B ** -0.5),
        'rg_w_gate': nrm((N_RGLRU, d, E_C), d ** -0.5),
        'rg_w_x': nrm((N_RGLRU, d, E_C), d ** -0.5),
        'rg_conv_w': nrm((N_RGLRU, CONV_W, E_C), CONV_W ** -0.5),
        'rg_conv_b': nrm((N_RGLRU, E_C), 0.02),
        'rg_w_a': nrm((N_RGLRU, NB_C, DB_C, DB_C), DB_C ** -0.5),
        'rg_b_a': nrm((N_RGLRU, E_C), 0.02),
        'rg_w_i': nrm((N_RGLRU, NB_C, DB_C, DB_C), DB_C ** -0.5),
        'rg_b_i': nrm((N_RGLRU, E_C), 0.02),
        'rg_lam': jnp.log(a_init) - jnp.log1p(-a_init),
        'rg_w_out': nrm((N_RGLRU, E_C, d), E_C ** -0.5),
        'ffd_w1': nrm((N_DENSE, d, D_FF), d ** -0.5),
        'ffd_w3': nrm((N_DENSE, d, D_FF), d ** -0.5),
        'ffd_w2': nrm((N_DENSE, D_FF, d), D_FF ** -0.5),
        'moe_router': nrm((N_MOE, d, N_EXPERTS), d ** -0.5),
        'moe_w1': nrm((N_MOE, N_EXPERTS, d, D_FF_E), d ** -0.5),
        'moe_w3': nrm((N_MOE, N_EXPERTS, d, D_FF_E), d ** -0.5),
        'moe_w2': nrm((N_MOE, N_EXPERTS, D_FF_E, d), D_FF_E ** -0.5),
    }


def reference(x_prompt, x_sample, cache_mla_kv, cache_mla_kscale, state_rglru_h, state_rglru_conv,
              page_table, g_mix, g_ffn, mla_wq_a, mla_g_qa, mla_wq_b, mla_wkv_a, mla_g_kva,
              mla_w_uk, mla_w_uv, mla_wo, mla_g_qn, mla_g_qr, mla_g_kn, mla_g_kr,
              gm_w_in, gm_g_v, gm_w_s, gm_b_s, gm_w_out,
              rg_w_gate, rg_w_x, rg_conv_w, rg_conv_b, rg_w_a, rg_b_a, rg_w_i, rg_b_i, rg_lam, rg_w_out,
              ffd_w1, ffd_w3, ffd_w2, moe_router, moe_w1, moe_w3, moe_w2):
    y_p, y_s = x_prompt, x_sample
    pos_p = jnp.arange(SEQ, dtype=jnp.int32)
    pos_s = PAST_LEN + jnp.arange(DEC_SEQ, dtype=jnp.int32)
    n_seq, n_pages = page_table.shape
    past_len = n_pages * PAGE_SIZE
    mask_s = jnp.arange(past_len + DEC_SEQ)[None, :] <= (past_len + jnp.arange(DEC_SEQ))[:, None]

    kv_p_out, ks_p_out, kv_s_out, ks_s_out, v_s_out = [], [], [], [], []
    h_p_out, conv_p_out, h_s_out, conv_s_out = [], [], [], []
    for layer in range(DEPTH):
        kind, j = MIXER_OF_LAYER[layer], MIXER_ORDINAL[layer]
        n_p = rmsnorm(y_p, g_mix[layer])
        n_s = rmsnorm(y_s, g_mix[layer])
        if kind == 0:
            w = (mla_wq_a[j], mla_g_qa[j], mla_wq_b[j], mla_wkv_a[j], mla_g_kva[j], mla_w_uk[j],
                 mla_g_qn[j], mla_g_qr[j], mla_g_kn[j], mla_g_kr[j])
            q_lat, q_pe, kv_row, k_scale = mla_project(n_p, pos_p, *w)
            mix_p = mla_output(mla_prompt_attend(q_lat, q_pe, kv_row, k_scale), mla_w_uv[j], mla_wo[j])
            kv_p_out.append(kv_row)
            ks_p_out.append(k_scale)
            q_lat, q_pe, kv_row, k_scale = mla_project(n_s, pos_s, *w)
            past_kv = cache_mla_kv[j, page_table].reshape(n_seq, past_len, KV_RANK + ROPE_DIM)
            past_ks = cache_mla_kscale[j, page_table].reshape(n_seq, past_len, N_HEADS)
            kv_all = jnp.concatenate([past_kv.astype(kv_row.dtype), kv_row], axis=1)
            ks_all = jnp.concatenate([past_ks.astype(k_scale.dtype), k_scale], axis=1)
            mix_s = mla_output(mla_attend(q_lat, q_pe, kv_all, ks_all, mask_s), mla_w_uv[j], mla_wo[j])
            kv_s_out.append(kv_row)
            ks_s_out.append(k_scale)
        elif kind == 1:
            w = (gm_w_in[j], gm_g_v[j], gm_w_s[j], gm_b_s[j], gm_w_out[j])
            mix_p, _ = gmlp_mix(n_p, *w)
            mix_s, v_rows = gmlp_mix(n_s, *w)
            v_s_out.append(v_rows)
        else:
            w = (rg_w_gate[j], rg_w_x[j], rg_conv_w[j], rg_conv_b[j], rg_w_a[j], rg_b_a[j],
                 rg_w_i[j], rg_b_i[j], rg_lam[j], rg_w_out[j])
            h0_p = jnp.zeros((n_p.shape[0], E_C), n_p.dtype)
            buf0_p = jnp.zeros((n_p.shape[0], CONV_W - 1, E_C), n_p.dtype)
            mix_p, h_last, buf = rglru_mix(n_p, h0_p, buf0_p, *w)
            h_p_out.append(h_last)
            conv_p_out.append(buf)
            mix_s, h_last, buf = rglru_mix(n_s, state_rglru_h[j], state_rglru_conv[j], *w)
            h_s_out.append(h_last)
            conv_s_out.append(buf)
        y_p = y_p + mix_p
        y_s = y_s + mix_s
        f_p = rmsnorm(y_p, g_ffn[layer])
        f_s = rmsnorm(y_s, g_ffn[layer])
        i = layer // 2
        if layer % 2 == 0:
            y_p = y_p + swiglu(f_p, ffd_w1[i], ffd_w3[i], ffd_w2[i])
            y_s = y_s + swiglu(f_s, ffd_w1[i], ffd_w3[i], ffd_w2[i])
        else:
            y_p = y_p + moe_ffn(f_p, moe_router[i], moe_w1[i], moe_w3[i], moe_w2[i])
            y_s = y_s + moe_ffn(f_s, moe_router[i], moe_w1[i], moe_w3[i], moe_w2[i])

    new_mla_kv_prompt = jnp.stack(kv_p_out)
    new_mla_kscale_prompt = jnp.stack(ks_p_out)
    new_mla_kv_sample = jnp.stack(kv_s_out)
    new_mla_kscale_sample = jnp.stack(ks_s_out)
    new_gmlp_v_sample = jnp.stack(v_s_out)
    new_rglru_h_prompt = jnp.stack(h_p_out)
    new_rglru_conv_prompt = jnp.stack(conv_p_out)
    new_rglru_h_sample = jnp.stack(h_s_out)
    new_rglru_conv_sample = jnp.stack(conv_s_out)
    return (y_p, y_s, new_mla_kv_prompt, new_mla_kscale_prompt, new_mla_kv_sample, new_mla_kscale_sample,
            new_gmlp_v_sample, new_rglru_h_prompt, new_rglru_conv_prompt, new_rglru_h_sample, new_rglru_conv_sample)
```

```python
import functools

import jax
import jax.numpy as jnp
from jax import lax
from jax.experimental import pallas as pl
from jax.experimental.pallas import tpu as pltpu

BF = jnp.bfloat16
F32 = jnp.float32
EPS = 1e-6
ROPE_THETA = 10000.0
C_RG = 8.0
TOP_K = 2
NEG_BIG = -1e30

LANES = 128
SUBLANES = 8
VMEM_LIMIT = 56 * 1024 * 1024


def _cparams(sem):
    return pltpu.CompilerParams(dimension_semantics=sem, vmem_limit_bytes=VMEM_LIMIT)


def _dot(a, b):
    return jnp.dot(a, b, preferred_element_type=F32)


def _dot_nt(a, b):
    return lax.dot_general(a, b, (((1,), (1,)), ((), ())), preferred_element_type=F32)


def _rms(x):
    return x * lax.rsqrt(jnp.mean(x * x, axis=-1, keepdims=True) + EPS)


def _rms_half(x):
    return x * lax.rsqrt(jnp.sum(x * x, axis=-1, keepdims=True) * (1.0 / 64.0) + EPS)


def _rope_half(x, cos, sin):
    lane = lax.broadcasted_iota(jnp.int32, x.shape, 1)
    rot = jnp.where(lane < 32, pltpu.roll(x, 96, 1), pltpu.roll(x, 32, 1))
    return x * cos + rot * sin


def _gelu(x):
    return 0.5 * x * (1.0 + jnp.tanh(0.7978845608028654 * (x + 0.044715 * (x * x * x))))


def _sigmoid(x):
    return 1.0 / (1.0 + jnp.exp(-x))


def _full(shape):
    nd = len(shape)
    return pl.BlockSpec(shape, lambda *_: (0,) * nd)


def _mla_proj_body(x_ref, cos_ref, sin_ref, gmix_ref, wqa_ref, gqa_ref, wqn_ref, wqr_ref,
                   wkvc_ref, wkvr_ref, gkva_ref, gkr_ref, wukf_ref, wukt_ref, gqn_ref, gqr_ref,
                   gkn_ref, qlat_ref, qpe_ref, ckv_ref, kpe_ref, ks_ref, *, n_heads):
    x = x_ref[...]
    n = (_rms(x) * gmix_ref[...]).astype(BF)
    cos = cos_ref[...]
    sin = sin_ref[...]
    cq = (_rms(_dot(n, wqa_ref[...])) * gqa_ref[...]).astype(BF)
    ckv = _rms(_dot(n, wkvc_ref[...])) * gkva_ref[...]
    ckv_ref[...] = ckv
    kr = _dot(n, wkvr_ref[...])
    kpe_ref[...] = _rope_half(_rms_half(kr) * gkr_ref[...], cos, sin)
    kn = _dot(ckv.astype(BF), wukf_ref[...])
    lane = lax.broadcasted_iota(jnp.int32, (x.shape[0], LANES), 1)
    ks = jnp.zeros((x.shape[0], LANES), F32)
    for h in range(n_heads):
        blk = kn[:, h * 128:(h + 1) * 128]
        ksh = lax.rsqrt(jnp.mean(blk * blk, axis=-1, keepdims=True) + EPS)
        ks = jnp.where(lane == h, ksh, ks)
    ks_ref[...] = ks
    qn = _dot(cq, wqn_ref[...])
    qr = _dot(cq, wqr_ref[...])
    gq = gqn_ref[...]
    gk = gkn_ref[...]
    for h in range(n_heads):
        a = (_rms(qn[:, h * 128:(h + 1) * 128]) * gq) * gk
        qlat_ref[h] = _dot(a.astype(BF), wukt_ref[h]).astype(BF)
        r = _rms_half(qr[:, h * 128:(h + 1) * 128]) * gqr_ref[...]
        qpe_ref[h] = _rope_half(r, cos, sin).astype(BF)


def _mla_proj(x, cos, sin, w, *, bm):
    t, d = x.shape
    h = w['wukt'].shape[0]
    kvr = w['wkvc'].shape[1]
    row = lambda i: (i, 0)
    in_specs = [pl.BlockSpec((bm, d), row), pl.BlockSpec((bm, LANES), row), pl.BlockSpec((bm, LANES), row)]
    names = ['gmix', 'wqa', 'gqa', 'wqn', 'wqr', 'wkvc', 'wkvr', 'gkva', 'gkr', 'wukf', 'wukt', 'gqn', 'gqr', 'gkn']
    in_specs += [_full(w[k].shape) for k in names]
    out_shape = (jax.ShapeDtypeStruct((h, t, kvr), BF), jax.ShapeDtypeStruct((h, t, LANES), BF),
                 jax.ShapeDtypeStruct((t, kvr), F32), jax.ShapeDtypeStruct((t, LANES), F32),
                 jax.ShapeDtypeStruct((t, LANES), F32))
    out_specs = (pl.BlockSpec((h, bm, kvr), lambda i: (0, i, 0)), pl.BlockSpec((h, bm, LANES), lambda i: (0, i, 0)),
                 pl.BlockSpec((bm, kvr), row), pl.BlockSpec((bm, LANES), row), pl.BlockSpec((bm, LANES), row))
    return pl.pallas_call(
        functools.partial(_mla_proj_body, n_heads=h),
        grid=(t // bm,), in_specs=in_specs, out_specs=out_specs, out_shape=out_shape,
        compiler_params=_cparams(("parallel",)), name="mla_proj",
    )(x, cos, sin, *[w[k] for k in names])


def _mla_out_rows(o_heads, wuv_ref, wo_ref, obuf_ref):
    for h, oh in enumerate(o_heads):
        v = _dot(oh.astype(BF), wuv_ref[h])
        obuf_ref[:, h * 128:(h + 1) * 128] = v.astype(BF)
    return _dot(obuf_ref[...], wo_ref[...])


def _mla_out_body(o_ref, x_ref, wuv_ref, wo_ref, y_ref, obuf_ref, *, n_heads, kvr):
    o = o_ref[...]
    heads = [o[:, h * kvr:(h + 1) * kvr] for h in range(n_heads)]
    y_ref[...] = x_ref[...] + _mla_out_rows(heads, wuv_ref, wo_ref, obuf_ref)


def _mla_out(o, x, wuv, wo, *, bm):
    t, d = x.shape
    h, kvr, vd = wuv.shape
    return pl.pallas_call(
        functools.partial(_mla_out_body, n_heads=h, kvr=kvr),
        grid=(t // bm,),
        in_specs=[pl.BlockSpec((bm, h * kvr), lambda i: (i, 0)), pl.BlockSpec((bm, d), lambda i: (i, 0)),
                  _full(wuv.shape), _full(wo.shape)],
        out_specs=pl.BlockSpec((bm, d), lambda i: (i, 0)),
        out_shape=jax.ShapeDtypeStruct((t, d), F32),
        scratch_shapes=[pltpu.VMEM((bm, h * vd), BF)],
        compiler_params=_cparams(("parallel",)), name="mla_out",
    )(o, x, wuv, wo)


def _flash_body(qlat_ref, qpe_ref, c_ref, kpe_ref, kst_ref, x_ref, wuv_ref, wo_ref, y_ref,
                m_sc, l_sc, acc_sc, obuf_ref, *, tq, tk, n_heads, scale):
    i = pl.program_id(1)
    j = pl.program_id(2)
    nj = pl.num_programs(2)
    last_needed = (i * tq + tq - 1) // tk
    rows = n_heads * tq

    @pl.when(j == 0)
    def _():
        m_sc[...] = jnp.full(m_sc.shape, NEG_BIG, F32)
        l_sc[...] = jnp.zeros(l_sc.shape, F32)
        acc_sc[...] = jnp.zeros(acc_sc.shape, F32)

    @pl.when(j <= last_needed)
    def _():
        q = qlat_ref[...].reshape(rows, qlat_ref.shape[-1])
        qp = qpe_ref[...].reshape(rows, qpe_ref.shape[-1])
        c = c_ref[...]
        s_n = _dot_nt(q, c)
        s_p = _dot_nt(qp, kpe_ref[...])
        kst = kst_ref[...]
        qpos = i * tq + lax.broadcasted_iota(jnp.int32, (tq, tk), 0)
        kpos = j * tk + lax.broadcasted_iota(jnp.int32, (tq, tk), 1)
        keep = kpos <= qpos
        parts = []
        for h in range(n_heads):
            sh = s_n[h * tq:(h + 1) * tq] * kst[h:h + 1, :] + s_p[h * tq:(h + 1) * tq]
            parts.append(jnp.where(keep, sh, NEG_BIG))
        s = jnp.concatenate(parts, axis=0)
        m_prev = m_sc[...]
        m_cur = jnp.maximum(m_prev, jnp.max(s, axis=1, keepdims=True))
        alpha = jnp.exp((m_prev - m_cur) * scale)
        p = jnp.exp((s - m_cur) * scale)
        l_sc[...] = alpha * l_sc[...] + jnp.sum(p, axis=1, keepdims=True)
        acc_sc[...] = alpha * acc_sc[...] + _dot(p.astype(BF), c)
        m_sc[...] = m_cur

    @pl.when(j == nj - 1)
    def _():
        inv = 1.0 / l_sc[...]
        heads = [acc_sc[h * tq:(h + 1) * tq, :] * inv[h * tq:(h + 1) * tq] for h in range(n_heads)]
        y_ref[...] = x_ref[...] + _mla_out_rows(heads, wuv_ref, wo_ref, obuf_ref)


def _flash_prompt(qlat, qpe, c_bf, kpe_bf, kst, x, wuv, wo, *, batch, seq, tq, tk, scale):
    h, _, kvr = qlat.shape
    d = x.shape[1]
    nq = seq // tq
    nk = seq // tk

    def kidx(b, i, j):
        return b * nk + jnp.minimum(j, (i * tq + tq - 1) // tk)

    in_specs = [
        pl.BlockSpec((h, tq, kvr), lambda b, i, j: (0, b * nq + i, 0)),
        pl.BlockSpec((h, tq, LANES), lambda b, i, j: (0, b * nq + i, 0)),
        pl.BlockSpec((tk, kvr), lambda b, i, j: (kidx(b, i, j), 0)),
        pl.BlockSpec((tk, LANES), lambda b, i, j: (kidx(b, i, j), 0)),
        pl.BlockSpec((h, tk), lambda b, i, j: (0, kidx(b, i, j))),
        pl.BlockSpec((tq, d), lambda b, i, j: (b * nq + i, 0)),
        _full(wuv.shape), _full(wo.shape),
    ]
    return pl.pallas_call(
        functools.partial(_flash_body, tq=tq, tk=tk, n_heads=h, scale=scale),
        grid=(batch, nq, nk), in_specs=in_specs,
        out_specs=pl.BlockSpec((tq, d), lambda b, i, j: (b * nq + i, 0)),
        out_shape=jax.ShapeDtypeStruct((batch * seq, d), F32),
        scratch_shapes=[pltpu.VMEM((h * tq, 1), F32), pltpu.VMEM((h * tq, 1), F32),
                        pltpu.VMEM((h * tq, kvr), F32), pltpu.VMEM((tq, wo.shape[0]), BF)],
        compiler_params=_cparams(("parallel", "parallel", "arbitrary")), name="mla_flash_prompt",
    )(qlat, qpe, c_bf, kpe_bf, kst, x, wuv, wo)


def _decode_body(pt_ref, q_ref, qp_ref, *rest, pps, page, n_heads, n_new, kvr, rope, scale):
    kv_refs = rest[:pps]
    ks_refs = rest[pps:2 * pps]
    newkv_ref, newks_ref, o_ref, kbuf, kst_sc, m_sc, l_sc, acc_sc = rest[2 * pps:]
    cidx = pl.program_id(1)
    nc = pl.num_programs(1)
    rows = n_heads * n_new

    @pl.when(cidx == 0)
    def _():
        m_sc[...] = jnp.full(m_sc.shape, NEG_BIG, F32)
        l_sc[...] = jnp.zeros(l_sc.shape, F32)
        acc_sc[...] = jnp.zeros(acc_sc.shape, F32)

    q = q_ref[...]
    qp = qp_ref[:, :rope]

    def attend(k_all, kst, keep):
        c = k_all[:, :kvr]
        s_n = _dot_nt(q, c)
        s_p = _dot_nt(qp, k_all[:, kvr:kvr + rope])
        parts = []
        for h in range(n_heads):
            sl = slice(h * n_new, (h + 1) * n_new)
            parts.append(s_n[sl] * kst[h:h + 1, :] + s_p[sl])
        s = jnp.concatenate(parts, axis=0)
        if keep is not None:
            s = jnp.where(keep, s, NEG_BIG)
        m_prev = m_sc[...]
        m_cur = jnp.maximum(m_prev, jnp.max(s, axis=1, keepdims=True))
        alpha = jnp.exp((m_prev - m_cur) * scale)
        p = jnp.exp((s - m_cur) * scale)
        l_sc[...] = alpha * l_sc[...] + jnp.sum(p, axis=1, keepdims=True)
        acc_sc[...] = alpha * acc_sc[...] + _dot(p.astype(BF), c)
        m_sc[...] = m_cur

    for p in range(pps):
        kbuf[p * page:(p + 1) * page, :] = kv_refs[p][...].astype(BF)
        kst_sc[:, p * page:(p + 1) * page] = ks_refs[p][...]
    attend(kbuf[...], kst_sc[...], None)

    @pl.when(cidx == nc - 1)
    def _():
        r = lax.broadcasted_iota(jnp.int32, (rows, page), 0)
        kj = lax.broadcasted_iota(jnp.int32, (rows, page), 1)
        keep = kj <= (r % n_new)
        attend(newkv_ref[...].astype(BF), newks_ref[...], keep)
        o_ref[...] = acc_sc[...] * (1.0 / l_sc[...])


def _decode_attend(page_table, qlat_s, qpe_s, cache_kv, j, cache_kst_j, newkv, newkst, *, pps, scale):
    bd, n_pages = page_table.shape
    rows, kvr = qlat_s.shape[1:]
    page, width = cache_kv.shape[2:]
    n_heads = cache_kst_j.shape[1]
    n_new = rows // n_heads
    rope = width - kvr
    nc = n_pages // pps
    pt_flat = page_table.reshape(-1)

    def page_map(p):
        return lambda b, c, pt: (pt[b * n_pages + c * pps + p], 0, 0)

    def page_map_kv(p):
        return lambda b, c, pt: (j, pt[b * n_pages + c * pps + p], 0, 0)

    in_specs = [pl.BlockSpec((None, rows, kvr), lambda b, c, pt: (b, 0, 0)),
                pl.BlockSpec((None, rows, LANES), lambda b, c, pt: (b, 0, 0))]
    in_specs += [pl.BlockSpec((None, None, page, width), page_map_kv(p)) for p in range(pps)]
    in_specs += [pl.BlockSpec((None, n_heads, page), page_map(p)) for p in range(pps)]
    in_specs += [pl.BlockSpec((None, page, width), lambda b, c, pt: (b, 0, 0)),
                 pl.BlockSpec((None, n_heads, page), lambda b, c, pt: (b, 0, 0))]
    grid_spec = pltpu.PrefetchScalarGridSpec(
        num_scalar_prefetch=1, grid=(bd, nc), in_specs=in_specs,
        out_specs=pl.BlockSpec((None, rows, kvr), lambda b, c, pt: (b, 0, 0)),
        scratch_shapes=[pltpu.VMEM((pps * page, width), BF), pltpu.VMEM((n_heads, pps * page), F32),
                        pltpu.VMEM((rows, 1), F32), pltpu.VMEM((rows, 1), F32), pltpu.VMEM((rows, kvr), F32)])
    return pl.pallas_call(
        functools.partial(_decode_body, pps=pps, page=page, n_heads=n_heads, n_new=n_new, kvr=kvr,
                          rope=rope, scale=scale),
        grid_spec=grid_spec, out_shape=jax.ShapeDtypeStruct((bd, rows, kvr), F32),
        compiler_params=_cparams(("parallel", "arbitrary")), name="mla_decode",
    )(pt_flat, qlat_s, qpe_s, *([cache_kv] * pps), *([cache_kst_j] * pps), newkv, newkst)


def _ffn_dense_body(x_ref, g_ref, w1_ref, w3_ref, w2_ref, o_ref, h_sc):
    j = pl.program_id(1)

    @pl.when(j == 0)
    def _():
        x = x_ref[...]
        h_sc[...] = (_rms(x) * g_ref[...]).astype(BF)
        o_ref[...] = x

    h = h_sc[...]
    a = _dot(h, w1_ref[...])
    b = _dot(h, w3_ref[...])
    g = (a * _sigmoid(a)) * b
    o_ref[...] += _dot(g.astype(BF), w2_ref[...])


def _ffn_dense(x, g, w1, w3, w2, *, bm, bf):
    t, d = x.shape
    ff = w1.shape[1]
    return pl.pallas_call(
        _ffn_dense_body, grid=(t // bm, ff // bf),
        in_specs=[pl.BlockSpec((bm, d), lambda i, j: (i, 0)), _full(g.shape),
                  pl.BlockSpec((d, bf), lambda i, j: (0, j)), pl.BlockSpec((d, bf), lambda i, j: (0, j)),
                  pl.BlockSpec((bf, d), lambda i, j: (j, 0))],
        out_specs=pl.BlockSpec((bm, d), lambda i, j: (i, 0)),
        out_shape=jax.ShapeDtypeStruct((t, d), F32),
        scratch_shapes=[pltpu.VMEM((bm, d), BF)],
        compiler_params=_cparams(("parallel", "arbitrary")), name="ffn_dense",
    )(x, g, w1, w3, w2)


def _ffn_group_body(te_ref, nt_ref, x_ref, gate_ref, w1_ref, w3_ref, w2_ref, o_ref):
    i = pl.program_id(0)
    j = pl.program_id(1)
    nj = pl.num_programs(1)

    @pl.when(i < nt_ref[0])
    def _():
        @pl.when(j == 0)
        def _():
            o_ref[...] = jnp.zeros(o_ref.shape, F32)

        h = x_ref[...]
        a = _dot(h, w1_ref[...])
        b = _dot(h, w3_ref[...])
        g = (a * _sigmoid(a)) * b
        o_ref[...] += _dot(g.astype(BF), w2_ref[...])

        @pl.when(j == nj - 1)
        def _():
            o_ref[...] = o_ref[...] * gate_ref[...]


def _ffn_grouped(tile_expert, n_tiles_used, xs, gates, w1, w3, w2, *, bm, bf):
    r, d = xs.shape
    ff = w1.shape[2]
    grid_spec = pltpu.PrefetchScalarGridSpec(
        num_scalar_prefetch=2, grid=(r // bm, ff // bf),
        in_specs=[pl.BlockSpec((bm, d), lambda i, j, te, nt: (i, 0)),
                  pl.BlockSpec((bm, 1), lambda i, j, te, nt: (i, 0)),
                  pl.BlockSpec((None, d, bf), lambda i, j, te, nt: (te[i], 0, j)),
                  pl.BlockSpec((None, d, bf), lambda i, j, te, nt: (te[i], 0, j)),
                  pl.BlockSpec((None, bf, d), lambda i, j, te, nt: (te[i], j, 0))],
        out_specs=pl.BlockSpec((bm, d), lambda i, j, te, nt: (i, 0)))
    return pl.pallas_call(
        _ffn_group_body, grid_spec=grid_spec, out_shape=jax.ShapeDtypeStruct((r, d), F32),
        compiler_params=_cparams(("parallel", "arbitrary")), name="ffn_grouped",
    )(tile_expert, n_tiles_used, xs, gates, w1, w3, w2)


def _router_body(x_ref, g_ref, wr_ref, f_ref, lg_ref):
    f = (_rms(x_ref[...]) * g_ref[...]).astype(BF)
    f_ref[...] = f
    lg_ref[...] = _dot(f, wr_ref[...])


def _router(x, g, wr, *, bm):
    t, d = x.shape
    return pl.pallas_call(
        _router_body, grid=(t // bm,),
        in_specs=[pl.BlockSpec((bm, d), lambda i: (i, 0)), _full(g.shape), _full(wr.shape)],
        out_specs=(pl.BlockSpec((bm, d), lambda i: (i, 0)), pl.BlockSpec((bm, LANES), lambda i: (i, 0))),
        out_shape=(jax.ShapeDtypeStruct((t, d), BF), jax.ShapeDtypeStruct((t, LANES), F32)),
        compiler_params=_cparams(("parallel",)), name="moe_router",
    )(x, g, wr)


def _gmlp_body(x_ref, g_ref, win_ref, gv_ref, ws_ref, bs_ref, wout_ref, y_ref, *maybe_v, chunk, groups):
    x = x_ref[...]
    bm = x.shape[0]
    n = (_rms(x) * g_ref[...]).astype(BF)
    z = _gelu(_dot(n, win_ref[...]))
    e = z.shape[1] // 2
    dg = e // groups
    u = z[:, :e]
    v = _rms(z[:, e:]) * gv_ref[...]
    if maybe_v:
        maybe_v[0][...] = v
    vb = v.astype(BF)
    bs = bs_ref[...]
    rows = []
    for c in range(bm // chunk):
        cols = []
        for g in range(groups):
            mixed = _dot(ws_ref[g], vb[c * chunk:(c + 1) * chunk, g * dg:(g + 1) * dg]) + bs[:, g:g + 1]
            cols.append(u[c * chunk:(c + 1) * chunk, g * dg:(g + 1) * dg] * mixed)
        rows.append(jnp.concatenate(cols, axis=1))
    y = jnp.concatenate(rows, axis=0) if len(rows) > 1 else rows[0]
    y_ref[...] = x + _dot(y.astype(BF), wout_ref[...])


def _gmlp(x, g, win, gv, ws, bs, wout, *, bm, emit_v):
    t, d = x.shape
    e = wout.shape[0]
    groups, chunk, _ = ws.shape
    out_shape = [jax.ShapeDtypeStruct((t, d), F32)]
    out_specs = [pl.BlockSpec((bm, d), lambda i: (i, 0))]
    if emit_v:
        out_shape.append(jax.ShapeDtypeStruct((t, e), F32))
        out_specs.append(pl.BlockSpec((bm, e), lambda i: (i, 0)))
    res = pl.pallas_call(
        functools.partial(_gmlp_body, chunk=chunk, groups=groups),
        grid=(t // bm,),
        in_specs=[pl.BlockSpec((bm, d), lambda i: (i, 0)), _full(g.shape), _full(win.shape), _full(gv.shape),
                  _full(ws.shape), _full(bs.shape), _full(wout.shape)],
        out_specs=tuple(out_specs), out_shape=tuple(out_shape),
        compiler_params=_cparams(("parallel",)), name="gmlp_v" if emit_v else "gmlp",
    )(x, g, win, gv, ws, bs, wout)
    return res


def _scan8(a8, u8):
    row = lax.broadcasted_iota(jnp.int32, a8.shape, 0)
    A, B = a8, u8
    for d in (1, 2, 4):
        a_sh = pltpu.roll(A, d, 0)
        b_sh = pltpu.roll(B, d, 0)
        m = row >= d
        B = jnp.where(m, A * b_sh + B, B)
        A = jnp.where(m, A * a_sh, A)
    return A, B


def _rglru_body(x_ref, g_ref, h0_ref, buf_ref, wgate_ref, wx_ref, cw_ref, cb_ref, wa_ref, ba_ref,
                wi_ref, bi_ref, sp_ref, wout_ref, y_ref, hl_ref, xl_ref,
                xe_sc, a_sc, u_sc, hs_sc, carry_sc, *, segmented, gw):
    bt = x_ref.shape[0]
    ec = wx_ref.shape[1]
    x = x_ref[...]
    n = (_rms(x) * g_ref[...]).astype(BF)
    gate = _gelu(_dot(n, wgate_ref[...]))
    xb = _dot(n, wx_ref[...])
    cw = cw_ref[...]
    cb = cb_ref[...]
    ngr = bt // SUBLANES

    if segmented:
        xe_sc[SUBLANES:, :] = xb
        xl_ref[...] = xb

        def conv_group(gi, _):
            base = pl.multiple_of(gi * SUBLANES, SUBLANES)
            cur = xe_sc[pl.ds(SUBLANES + base, SUBLANES), :]
            hist = buf_ref[gi]
            a_sc[0:SUBLANES, :] = hist
            a_sc[SUBLANES:2 * SUBLANES, :] = cur
            acc = cb + cw[3:4, :] * cur
            for dly in (1, 2, 3):
                acc = acc + cw[3 - dly:4 - dly, :] * a_sc[pl.ds(SUBLANES - dly, SUBLANES), :]
            u_sc[pl.ds(base, SUBLANES), :] = acc
            return 0

        lax.fori_loop(0, ngr, conv_group, 0)
        xc = u_sc[...]
    else:
        t_idx = pl.program_id(1)

        @pl.when(t_idx == 0)
        def _():
            xe_sc[0:SUBLANES, :] = jnp.zeros((SUBLANES, ec), F32)
            carry_sc[...] = jnp.zeros(carry_sc.shape, F32)

        xe_sc[SUBLANES:, :] = xb
        xc = cb + cw[3:4, :] * xb
        for dly in (1, 2, 3):
            xc = xc + cw[3 - dly:4 - dly, :] * xe_sc[pl.ds(SUBLANES - dly, bt), :]
        xl_ref[...] = xb[bt - SUBLANES:, :]
        xe_sc[0:SUBLANES, :] = xb[bt - SUBLANES:, :]

    xcb = xc.astype(BF)
    ra, ri = [], []
    for k in range(ec // gw):
        blk = xcb[:, k * gw:(k + 1) * gw]
        ra.append(_dot(blk, wa_ref[k]))
        ri.append(_dot(blk, wi_ref[k]))
    r = _sigmoid(jnp.concatenate(ra, axis=1) + ba_ref[...])
    ig = _sigmoid(jnp.concatenate(ri, axis=1) + bi_ref[...])
    log_a = (-C_RG) * r * sp_ref[...]
    a = jnp.exp(log_a)
    mult = jnp.sqrt(-jnp.tanh(log_a) * (a * a + 1.0))
    a_sc[0:bt, :] = a
    u_sc[...] = mult * ig * xc

    if segmented:
        def scan_group(gi, _):
            base = pl.multiple_of(gi * SUBLANES, SUBLANES)
            A, B = _scan8(a_sc[pl.ds(base, SUBLANES), :], u_sc[pl.ds(base, SUBLANES), :])
            h8 = A * h0_ref[pl.ds(gi, 1), :] + B
            hs_sc[pl.ds(base, SUBLANES), :] = h8
            hl_ref[pl.ds(gi, 1), :] = h8[SUBLANES - 1:SUBLANES, :]
            return 0

        lax.fori_loop(0, ngr, scan_group, 0)
    else:
        def scan_group(gi, carry):
            base = pl.multiple_of(gi * SUBLANES, SUBLANES)
            A, B = _scan8(a_sc[pl.ds(base, SUBLANES), :], u_sc[pl.ds(base, SUBLANES), :])
            h8 = A * carry + B
            hs_sc[pl.ds(base, SUBLANES), :] = h8
            return h8[SUBLANES - 1:SUBLANES, :]

        last = lax.fori_loop(0, ngr, scan_group, carry_sc[...])
        carry_sc[...] = last
        hl_ref[...] = last

    y_ref[...] = x + _dot((hs_sc[...] * gate).astype(BF), wout_ref[...])


def _rglru(x, g, h0, buf8, w, *, bt, batch, seq, segmented):
    d = x.shape[1]
    ec = w['wx'].shape[1]
    gw = w['wa'].shape[1]
    names = ['wgate', 'wx', 'cw', 'cb', 'wa', 'ba', 'wi', 'bi', 'sp', 'wout']
    wspecs = [_full(w[k].shape) for k in names]
    if segmented:
        t = x.shape[0]
        nseq = t // SUBLANES
        sb = bt // SUBLANES
        grid = (t // bt,)
        in_specs = [pl.BlockSpec((bt, d), lambda i: (i, 0)), _full(g.shape),
                    pl.BlockSpec((sb, ec), lambda i: (i, 0)),
                    pl.BlockSpec((sb, SUBLANES, ec), lambda i: (i, 0, 0))] + wspecs
        out_shape = (jax.ShapeDtypeStruct((t, d), F32), jax.ShapeDtypeStruct((nseq, ec), F32),
                     jax.ShapeDtypeStruct((t, ec), F32))
        out_specs = (pl.BlockSpec((bt, d), lambda i: (i, 0)), pl.BlockSpec((sb, ec), lambda i: (i, 0)),
                     pl.BlockSpec((bt, ec), lambda i: (i, 0)))
        sem = ("parallel",)
    else:
        nt = seq // bt
        grid = (batch, nt)
        in_specs = [pl.BlockSpec((bt, d), lambda b, i: (b * nt + i, 0)), _full(g.shape), _full(h0.shape),
                    _full(buf8.shape)] + wspecs
        out_shape = (jax.ShapeDtypeStruct((batch * seq, d), F32), jax.ShapeDtypeStruct((batch, 1, ec), F32),
                     jax.ShapeDtypeStruct((batch, SUBLANES, ec), F32))
        out_specs = (pl.BlockSpec((bt, d), lambda b, i: (b * nt + i, 0)),
                     pl.BlockSpec((None, 1, ec), lambda b, i: (b, 0, 0)),
                     pl.BlockSpec((None, SUBLANES, ec), lambda b, i: (b, 0, 0)))
        sem = ("parallel", "arbitrary")
    scratch = [pltpu.VMEM((bt + SUBLANES, ec), F32), pltpu.VMEM((max(bt, 2 * SUBLANES), ec), F32),
               pltpu.VMEM((bt, ec), F32), pltpu.VMEM((bt, ec), F32), pltpu.VMEM((1, ec), F32)]
    return pl.pallas_call(
        functools.partial(_rglru_body, segmented=segmented, gw=gw),
        grid=grid, in_specs=in_specs, out_specs=out_specs, out_shape=out_shape, scratch_shapes=scratch,
        compiler_params=_cparams(sem), name="rglru_seg" if segmented else "rglru",
    )(x, g, h0, buf8, *[w[k] for k in names])


def _row(v):
    return v.reshape(1, -1).astype(F32)


def _pad_lanes(v, width=LANES):
    return jnp.pad(v, [(0, 0)] * (v.ndim - 1) + [(0, width - v.shape[-1])])


def _prep_mla(g_mix_l, wq_a, g_qa, wq_b, wkv_a, g_kva, w_uk, w_uv, wo, g_qn, g_qr, g_kn, g_kr):
    qr, h, dq = wq_b.shape
    kvr = w_uk.shape[0]
    nope = w_uk.shape[2]
    rope = dq - nope
    w = {
        'gmix': _row(g_mix_l),
        'wqa': wq_a.astype(BF), 'gqa': _row(g_qa),
        'wqn': wq_b[:, :, :nope].reshape(qr, h * nope).astype(BF),
        'wqr': _pad_lanes(wq_b[:, :, nope:]).reshape(qr, h * LANES).astype(BF),
        'wkvc': wkv_a[:, :kvr].astype(BF),
        'wkvr': _pad_lanes(wkv_a[:, kvr:]).astype(BF),
        'gkva': _row(g_kva), 'gkr': _pad_lanes(_row(g_kr)),
        'wukf': w_uk.reshape(kvr, h * nope).astype(BF),
        'wukt': jnp.transpose(w_uk, (1, 2, 0)).astype(BF),
        'gqn': _row(g_qn), 'gqr': _pad_lanes(_row(g_qr)), 'gkn': _row(g_kn),
    }
    wuv = jnp.transpose(w_uv, (1, 0, 2)).astype(BF)
    wo2 = wo.reshape(-1, wo.shape[-1]).astype(BF)
    return w, wuv, wo2, rope


def _rope_tables(pos, rope):
    half = rope // 2
    inv_freq = ROPE_THETA ** (-jnp.arange(half, dtype=F32) / half)
    ang = pos.astype(F32)[:, None] * inv_freq[None, :]
    cos, sin = jnp.cos(ang), jnp.sin(ang)
    cos_t = _pad_lanes(jnp.concatenate([cos, cos], axis=1))
    sin_t = _pad_lanes(jnp.concatenate([-sin, sin], axis=1))
    return cos_t, sin_t


def _mla_layer(x, j, cos_t, sin_t, prm, cache_kv, cache_ks, page_table, dims):
    batch, seq, bd, ns = dims
    tp = batch * seq
    w, wuv, wo2, rope = prm
    h, kvr, _ = wuv.shape
    scale = float(w['wukt'].shape[1] + rope) ** -0.5
    qlat, qpe, ckv, kpe, ks = _mla_proj(x, cos_t, sin_t, w, bm=512)
    kv_row = jnp.concatenate([ckv, kpe[:, :rope]], axis=1)
    ks8 = ks[:, :h]
    c_bf = ckv[:tp].astype(BF)
    kpe_bf = kpe[:tp].astype(BF)
    kst = jnp.transpose(ks8[:tp])
    y_p = _flash_prompt(qlat, qpe, c_bf, kpe_bf, kst, x, wuv, wo2, batch=batch, seq=seq, tq=128, tk=512,
                        scale=scale)
    page = cache_kv.shape[2]
    qlat_s = jnp.transpose(qlat[:, tp:].reshape(h, bd, ns, kvr), (1, 0, 2, 3)).reshape(bd, h * ns, kvr)
    qpe_s = jnp.transpose(qpe[:, tp:].reshape(h, bd, ns, LANES), (1, 0, 2, 3)).reshape(bd, h * ns, LANES)
    newkv = jnp.pad(kv_row[tp:].reshape(bd, ns, kvr + rope), ((0, 0), (0, page - ns), (0, 0)))
    newkst = jnp.pad(jnp.transpose(ks8[tp:].reshape(bd, ns, h), (0, 2, 1)), ((0, 0), (0, 0), (0, page - ns)))
    cache_kst = jnp.transpose(cache_ks[j], (0, 2, 1))
    o_s = _decode_attend(page_table, qlat_s, qpe_s, cache_kv, j, cache_kst, newkv, newkst, pps=16, scale=scale)
    o_s = jnp.transpose(o_s.reshape(bd, h, ns, kvr), (0, 2, 1, 3)).reshape(bd * ns, h * kvr)
    y_s = _mla_out(o_s, x[tp:], wuv, wo2, bm=min(512, bd * ns))
    y = jnp.concatenate([y_p, y_s], axis=0)
    return y, kv_row, ks8


def _gmlp_layer(x, g_mix_l, w_in, g_v, w_s, b_s, w_out, dims):
    batch, seq, bd, ns = dims
    tp = batch * seq
    groups, chunk, _ = w_s.shape
    tri = jnp.tril(jnp.ones((chunk, chunk), bool))
    ws_p = jnp.where(tri[None], w_s, 0).astype(BF)
    bs_p = _pad_lanes(jnp.transpose(b_s))
    l = min(ns, chunk)
    tri_s = jnp.tril(jnp.ones((l, l), bool))
    ws_small = jnp.where(tri_s[None], w_s[:, :l, :l], 0)
    eye = jnp.eye(chunk // l, dtype=F32)
    ws_s = jnp.einsum('ab,gts->gatbs', eye, ws_small).reshape(groups, chunk, chunk).astype(BF)
    bs_s = _pad_lanes(jnp.tile(jnp.transpose(b_s[:, :l]), (chunk // l, 1)))
    args = (_row(g_mix_l), w_in.astype(BF), _row(g_v))
    wout = w_out.astype(BF)
    (y_p,) = _gmlp(x[:tp], *args, ws_p, bs_p, wout, bm=256, emit_v=False)
    y_s, v_s = _gmlp(x[tp:], *args, ws_s, bs_s, wout, bm=256, emit_v=True)
    return jnp.concatenate([y_p, y_s], axis=0), v_s


def _rglru_layer(x, g_mix_l, h0_s, buf_s, w_gate, w_x, conv_w, conv_b, w_a, b_a, w_i, b_i, lam, w_out, dims):
    batch, seq, bd, ns = dims
    tp = batch * seq
    nb, db, _ = w_a.shape
    ec = nb * db
    pair = 2
    gw = db * pair

    def blockdiag(wb):
        wb = wb.reshape(nb // pair, pair, db, db)
        eye = jnp.eye(pair, dtype=wb.dtype)
        return jnp.einsum('kpde,pq->kpdqe', wb, eye).reshape(nb // pair, gw, gw).astype(BF)

    w = {'wgate': w_gate.astype(BF), 'wx': w_x.astype(BF), 'cw': _pad_rows(conv_w), 'cb': _row(conv_b),
         'wa': blockdiag(w_a), 'ba': _row(b_a), 'wi': blockdiag(w_i), 'bi': _row(b_i),
         'sp': _row(jax.nn.softplus(-lam.astype(F32))), 'wout': w_out.astype(BF)}
    g = _row(g_mix_l)
    cw = conv_w.shape[0]
    zeros_h = jnp.zeros((SUBLANES, ec), F32)
    zeros_b = jnp.zeros((1, SUBLANES, ec), F32)
    y_p, hl_p, xl_p = _rglru(x[:tp], g, zeros_h, zeros_b, w, bt=256, batch=batch, seq=seq, segmented=False)
    buf8 = jnp.pad(buf_s.astype(F32), ((0, 0), (SUBLANES - (cw - 1), 0), (0, 0)))
    y_s, hl_s, xl_s = _rglru(x[tp:], g, h0_s.astype(F32), buf8, w, bt=256, batch=bd, seq=ns, segmented=True)
    h_p = hl_p[:, 0, :]
    conv_p = xl_p[:, SUBLANES - (cw - 1):, :]
    conv_s = xl_s.reshape(bd, ns, ec)[:, ns - (cw - 1):, :]
    return jnp.concatenate([y_p, y_s], axis=0), h_p, conv_p, hl_s, conv_s


def _pad_rows(v, rows=SUBLANES):
    return jnp.pad(v.astype(F32), ((0, rows - v.shape[0]), (0, 0)))


def _moe_layer(x, g_ffn_l, router, w1, w3, w2, *, bm=512, bf=512):
    t, d = x.shape
    ne = router.shape[1]
    f, lg = _router(x, _row(g_ffn_l), _pad_lanes(router).astype(BF), bm=512)
    logits = lg[:, :ne]
    top_v, top_i = lax.top_k(logits, TOP_K)
    gates = jax.nn.softmax(top_v, axis=-1)
    flat_e = top_i.reshape(-1)
    onehot = (flat_e[:, None] == jnp.arange(ne)[None, :]).astype(jnp.int32)
    rank = jnp.cumsum(onehot, axis=0) - onehot
    counts = jnp.sum(onehot, axis=0)
    tiles_per = (counts + bm - 1) // bm
    tile_start = jnp.cumsum(tiles_per) - tiles_per
    dest = jnp.sum(onehot * (tile_start[None, :] * bm + rank), axis=1)
    n_tiles = (t * TOP_K) // bm + ne
    r_pad = n_tiles * bm
    tile_ids = jnp.arange(n_tiles)
    n_used = jnp.sum(tiles_per)
    tile_expert = jnp.sum((tile_ids[:, None] >= tile_start[None, :]).astype(jnp.int32), axis=1) - 1
    last_e = jnp.max(jnp.where(tiles_per > 0, jnp.arange(ne), 0))
    tile_expert = jnp.where(tile_ids < n_used, tile_expert, last_e).astype(jnp.int32)
    src_tok = jnp.repeat(jnp.arange(t), TOP_K)
    row_src = jnp.zeros((r_pad,), jnp.int32).at[dest].set(src_tok)
    row_gate = jnp.zeros((r_pad,), F32).at[dest].set(gates.reshape(-1))
    xs = jnp.take(f, row_src, axis=0)
    ys = _ffn_grouped(tile_expert, n_used.reshape(1).astype(jnp.int32), xs, row_gate.reshape(-1, 1),
                      w1.astype(BF), w3.astype(BF), w2.astype(BF), bm=bm, bf=bf)
    picked = jnp.take(ys, dest, axis=0).reshape(t, TOP_K, d)
    return x + jnp.sum(picked, axis=1)


def kernel(x_prompt, x_sample, cache_mla_kv, cache_mla_kscale, state_rglru_h, state_rglru_conv, page_table, g_mix, g_ffn, mla_wq_a, mla_g_qa, mla_wq_b, mla_wkv_a, mla_g_kva, mla_w_uk, mla_w_uv, mla_wo, mla_g_qn, mla_g_qr, mla_g_kn, mla_g_kr, gm_w_in, gm_g_v, gm_w_s, gm_b_s, gm_w_out, rg_w_gate, rg_w_x, rg_conv_w, rg_conv_b, rg_w_a, rg_b_a, rg_w_i, rg_b_i, rg_lam, rg_w_out, ffd_w1, ffd_w3, ffd_w2, moe_router, moe_w1, moe_w3, moe_w2):
    batch, seq, d = x_prompt.shape
    bd, ns, _ = x_sample.shape
    dims = (batch, seq, bd, ns)
    tp = batch * seq
    depth = g_mix.shape[0]
    n_mixers = 3
    past_len = page_table.shape[1] * cache_mla_kv.shape[2]

    x = jnp.concatenate([x_prompt.reshape(tp, d), x_sample.reshape(bd * ns, d)], axis=0)
    rope = mla_wq_b.shape[-1] - mla_w_uk.shape[-1]
    pos = jnp.concatenate([jnp.tile(jnp.arange(seq, dtype=jnp.int32), batch),
                           jnp.tile(past_len + jnp.arange(ns, dtype=jnp.int32), bd)])
    cos_t, sin_t = _rope_tables(pos, rope)

    kv_out, ks_out, v_out, hp_out, cp_out, hs_out, cs_out = [], [], [], [], [], [], []
    counts = [0, 0, 0]
    for layer in range(depth):
        kind = layer % n_mixers
        j = counts[kind]
        counts[kind] += 1
        if kind == 0:
            prm = _prep_mla(g_mix[layer], mla_wq_a[j], mla_g_qa[j], mla_wq_b[j], mla_wkv_a[j], mla_g_kva[j],
                            mla_w_uk[j], mla_w_uv[j], mla_wo[j], mla_g_qn[j], mla_g_qr[j], mla_g_kn[j],
                            mla_g_kr[j])
            x, kv_row, ks8 = _mla_layer(x, j, cos_t, sin_t, prm, cache_mla_kv, cache_mla_kscale, page_table, dims)
            kv_out.append(kv_row)
            ks_out.append(ks8)
        elif kind == 1:
            x, v_s = _gmlp_layer(x, g_mix[layer], gm_w_in[j], gm_g_v[j], gm_w_s[j], gm_b_s[j], gm_w_out[j], dims)
            v_out.append(v_s.reshape(bd, ns, -1))
        else:
            x, h_p, c_p, h_s, c_s = _rglru_layer(
                x, g_mix[layer], state_rglru_h[j], state_rglru_conv[j], rg_w_gate[j], rg_w_x[j], rg_conv_w[j],
                rg_conv_b[j], rg_w_a[j], rg_b_a[j], rg_w_i[j], rg_b_i[j], rg_lam[j], rg_w_out[j], dims)
            hp_out.append(h_p)
            cp_out.append(c_p)
            hs_out.append(h_s)
            cs_out.append(c_s)
        i = layer // 2
        if layer % 2 == 0:
            x = _ffn_dense(x, _row(g_ffn[layer]), ffd_w1[i].astype(BF), ffd_w3[i].astype(BF),
                           ffd_w2[i].astype(BF), bm=512, bf=1408)
        else:
            x = _moe_layer(x, g_ffn[layer], moe_router[i], moe_w1[i], moe_w3[i], moe_w2[i])

    kv_all = jnp.stack(kv_out)
    ks_all = jnp.stack(ks_out)
    return (x[:tp].reshape(batch, seq, d), x[tp:].reshape(bd, ns, d),
            kv_all[:, :tp].reshape(len(kv_out), batch, seq, -1), ks_all[:, :tp].reshape(len(ks_out), batch, seq, -1),
            kv_all[:, tp:].reshape(len(kv_out), bd, ns, -1), ks_all[:, tp:].reshape(len(ks_out), bd, ns, -1),
            jnp.stack(v_out), jnp.stack(hp_out), jnp.stack(cp_out), jnp.stack(hs_out), jnp.stack(cs_out))
```

```python
import functools

import jax
import jax.numpy as jnp
from jax import lax
from jax.experimental import pallas as pl
from jax.experimental.pallas import tpu as pltpu
from jax.experimental.pallas import tpu_sc as plsc

BF = jnp.bfloat16
F32 = jnp.float32
EPS = 1e-6
ROPE_THETA = 10000.0
C_RG = 8.0
TOP_K = 2
NEG_BIG = -1e30
LOG2E = 1.4426950408889634

LANES = 128
SUBLANES = 8
VMEM_LIMIT = 56 * 1024 * 1024

FLASH_TQ = 128
FLASH_TK = 1024
FLASH_HC = 2
DECODE_PPS = 32


def _cparams(sem):
    return pltpu.CompilerParams(dimension_semantics=sem, vmem_limit_bytes=VMEM_LIMIT)


def _dot(a, b):
    return jnp.dot(a, b, preferred_element_type=F32)


def _dot_nt(a, b):
    return lax.dot_general(a, b, (((1,), (1,)), ((), ())), preferred_element_type=F32)


def _rms(x):
    return x * lax.rsqrt(jnp.mean(x * x, axis=-1, keepdims=True) + EPS)


def _rms_half(x):
    return x * lax.rsqrt(jnp.sum(x * x, axis=-1, keepdims=True) * (1.0 / 64.0) + EPS)


def _rope_half(x, cos, sin):
    lane = lax.broadcasted_iota(jnp.int32, x.shape, 1)
    rot = jnp.where(lane < 32, pltpu.roll(x, 96, 1), pltpu.roll(x, 32, 1))
    return x * cos + rot * sin


def _gelu(x):
    return 0.5 * x * (1.0 + jnp.tanh(0.7978845608028654 * (x + 0.044715 * (x * x * x))))


def _sigmoid(x):
    return 1.0 / (1.0 + jnp.exp(-x))


def _full(shape):
    nd = len(shape)
    return pl.BlockSpec(shape, lambda *_: (0,) * nd)


def _mla_proj_body(x_ref, cos_ref, sin_ref, gmix_ref, wqa_ref, gqa_ref, wqn_ref, wqr_ref,
                   wkvc_ref, wkvr_ref, gkva_ref, gkr_ref, wukf_ref, wukt_ref, gqn_ref, gqr_ref,
                   gkn_ref, qlat_ref, qpe_ref, ckv_ref, kpe_ref, ks_ref, *, n_heads):
    x = x_ref[...]
    n = (_rms(x) * gmix_ref[...]).astype(BF)
    cos = cos_ref[...]
    sin = sin_ref[...]
    cq = (_rms(_dot(n, wqa_ref[...])) * gqa_ref[...]).astype(BF)
    ckv = _rms(_dot(n, wkvc_ref[...])) * gkva_ref[...]
    ckv_ref[...] = ckv
    kr = _dot(n, wkvr_ref[...])
    kpe_ref[...] = _rope_half(_rms_half(kr) * gkr_ref[...], cos, sin)
    kn = _dot(ckv.astype(BF), wukf_ref[...])
    lane = lax.broadcasted_iota(jnp.int32, (x.shape[0], LANES), 1)
    ks = jnp.zeros((x.shape[0], LANES), F32)
    for h in range(n_heads):
        blk = kn[:, h * 128:(h + 1) * 128]
        ksh = lax.rsqrt(jnp.mean(blk * blk, axis=-1, keepdims=True) + EPS)
        ks = jnp.where(lane == h, ksh, ks)
    ks_ref[...] = ks
    qn = _dot(cq, wqn_ref[...])
    qr = _dot(cq, wqr_ref[...])
    gq = gqn_ref[...]
    gk = gkn_ref[...]
    for h in range(n_heads):
        a = (_rms(qn[:, h * 128:(h + 1) * 128]) * gq) * gk
        qlat_ref[h] = _dot(a.astype(BF), wukt_ref[h]).astype(BF)
        r = _rms_half(qr[:, h * 128:(h + 1) * 128]) * gqr_ref[...]
        qpe_ref[h] = _rope_half(r, cos, sin).astype(BF)


def _mla_proj(x, cos, sin, w, *, bm):
    t, d = x.shape
    h = w['wukt'].shape[0]
    kvr = w['wkvc'].shape[1]
    row = lambda i: (i, 0)
    in_specs = [pl.BlockSpec((bm, d), row), pl.BlockSpec((bm, LANES), row), pl.BlockSpec((bm, LANES), row)]
    names = ['gmix', 'wqa', 'gqa', 'wqn', 'wqr', 'wkvc', 'wkvr', 'gkva', 'gkr', 'wukf', 'wukt', 'gqn', 'gqr', 'gkn']
    in_specs += [_full(w[k].shape) for k in names]
    out_shape = (jax.ShapeDtypeStruct((h, t, kvr), BF), jax.ShapeDtypeStruct((h, t, LANES), BF),
                 jax.ShapeDtypeStruct((t, kvr), F32), jax.ShapeDtypeStruct((t, LANES), F32),
                 jax.ShapeDtypeStruct((t, LANES), F32))
    out_specs = (pl.BlockSpec((h, bm, kvr), lambda i: (0, i, 0)), pl.BlockSpec((h, bm, LANES), lambda i: (0, i, 0)),
                 pl.BlockSpec((bm, kvr), row), pl.BlockSpec((bm, LANES), row), pl.BlockSpec((bm, LANES), row))
    return pl.pallas_call(
        functools.partial(_mla_proj_body, n_heads=h),
        grid=(t // bm,), in_specs=in_specs, out_specs=out_specs, out_shape=out_shape,
        compiler_params=_cparams(("parallel",)), name="mla_proj",
    )(x, cos, sin, *[w[k] for k in names])


def _mla_out_rows(o_heads, wuv_ref, wo_ref, obuf_ref):
    for h, oh in enumerate(o_heads):
        v = _dot(oh.astype(BF), wuv_ref[h])
        obuf_ref[:, h * 128:(h + 1) * 128] = v.astype(BF)
    return _dot(obuf_ref[...], wo_ref[...])


def _mla_out_body(o_ref, x_ref, wuv_ref, wo_ref, y_ref, obuf_ref, *, n_heads, kvr):
    o = o_ref[...]
    heads = [o[:, h * kvr:(h + 1) * kvr] for h in range(n_heads)]
    y_ref[...] = x_ref[...] + _mla_out_rows(heads, wuv_ref, wo_ref, obuf_ref)


def _mla_out(o, x, wuv, wo, *, bm):
    t, d = x.shape
    h, kvr, vd = wuv.shape
    return pl.pallas_call(
        functools.partial(_mla_out_body, n_heads=h, kvr=kvr),
        grid=(t // bm,),
        in_specs=[pl.BlockSpec((bm, h * kvr), lambda i: (i, 0)), pl.BlockSpec((bm, d), lambda i: (i, 0)),
                  _full(wuv.shape), _full(wo.shape)],
        out_specs=pl.BlockSpec((bm, d), lambda i: (i, 0)),
        out_shape=jax.ShapeDtypeStruct((t, d), F32),
        scratch_shapes=[pltpu.VMEM((bm, h * vd), BF)],
        compiler_params=_cparams(("parallel",)), name="mla_out",
    )(o, x, wuv, wo)


def _flash_body(qi_ref, kj_ref, qlat_ref, qpe_ref, c_ref, kpe_ref, kst_ref, x_ref, wuv_ref, wo_ref, y_ref,
                m_sc, l_sc, acc_sc, obuf_ref, *, tq, tk, n_heads, hc, c2):
    pidx = pl.program_id(1)
    i = qi_ref[pidx]
    j = kj_ref[pidx]
    last_needed = (i * tq + tq - 1) // tk
    kvr = qlat_ref.shape[-1]

    @pl.when(j == 0)
    def _():
        m_sc[...] = jnp.full(m_sc.shape, NEG_BIG, F32)
        l_sc[...] = jnp.zeros(l_sc.shape, F32)
        acc_sc[...] = jnp.zeros(acc_sc.shape, F32)

    def step(masked):
        c = c_ref[...]
        kpe = kpe_ref[...]
        kst = kst_ref[...] * c2
        if masked:
            qpos = i * tq + lax.broadcasted_iota(jnp.int32, (tq, tk), 0)
            kpos = j * tk + lax.broadcasted_iota(jnp.int32, (tq, tk), 1)
            keep = kpos <= qpos
        for g in range(n_heads // hc):
            r0, nr = g * hc * tq, hc * tq
            q = qlat_ref[g * hc:(g + 1) * hc].reshape(nr, kvr)
            qp = qpe_ref[g * hc:(g + 1) * hc].reshape(nr, LANES)
            s_n = _dot_nt(q, c)
            s_p = _dot_nt(qp, kpe)
            p_rows = []
            for hh in range(hc):
                h = g * hc + hh
                hr = slice(hh * tq, (hh + 1) * tq)
                pieces = []
                mx = None
                for kt in range(tk // LANES):
                    cols = slice(kt * LANES, (kt + 1) * LANES)
                    sh = s_n[hr, cols] * kst[h:h + 1, cols] + s_p[hr, cols] * c2
                    if masked:
                        sh = jnp.where(keep[:, cols], sh, NEG_BIG)
                    pieces.append(sh)
                    mx = sh if mx is None else jnp.maximum(mx, sh)
                hs = slice(h * tq, (h + 1) * tq)
                m_prev = m_sc[hs]
                m_cur = jnp.maximum(m_prev, jnp.max(mx, axis=1, keepdims=True))
                alpha = jnp.exp2(m_prev - m_cur)
                ps = []
                sm = None
                for sh in pieces:
                    p = jnp.exp2(sh - m_cur)
                    sm = p if sm is None else sm + p
                    ps.append(p.astype(BF))
                l_sc[hs] = alpha * l_sc[hs] + jnp.sum(sm, axis=1, keepdims=True)
                m_sc[hs] = m_cur
                acc_sc[hs] = alpha * acc_sc[hs]
                p_rows.append(jnp.concatenate(ps, axis=1))
            p_all = jnp.concatenate(p_rows, axis=0) if hc > 1 else p_rows[0]
            acc_sc[r0:r0 + nr] += _dot(p_all, c)

    @pl.when(j < last_needed)
    def _():
        step(False)

    @pl.when(j == last_needed)
    def _():
        step(True)
        inv = 1.0 / l_sc[...]
        heads = [acc_sc[h * tq:(h + 1) * tq, :] * inv[h * tq:(h + 1) * tq] for h in range(n_heads)]
        y_ref[...] = x_ref[...] + _mla_out_rows(heads, wuv_ref, wo_ref, obuf_ref)


def _flash_prompt(qlat, qpe, c_bf, kpe_bf, kst, x, wuv, wo, *, batch, seq, tq, tk, hc, scale):
    h, _, kvr = qlat.shape
    d = x.shape[1]
    nq = seq // tq
    nk = seq // tk
    pairs = [(i, j) for i in range(nq) for j in range((i * tq + tq - 1) // tk + 1)]
    qi = jnp.asarray([p[0] for p in pairs], jnp.int32)
    kj = jnp.asarray([p[1] for p in pairs], jnp.int32)
    in_specs = [
        pl.BlockSpec((h, tq, kvr), lambda b, p, qi, kj: (0, b * nq + qi[p], 0)),
        pl.BlockSpec((h, tq, LANES), lambda b, p, qi, kj: (0, b * nq + qi[p], 0)),
        pl.BlockSpec((tk, kvr), lambda b, p, qi, kj: (b * nk + kj[p], 0)),
        pl.BlockSpec((tk, LANES), lambda b, p, qi, kj: (b * nk + kj[p], 0)),
        pl.BlockSpec((h, tk), lambda b, p, qi, kj: (0, b * nk + kj[p])),
        pl.BlockSpec((tq, d), lambda b, p, qi, kj: (b * nq + qi[p], 0)),
        pl.BlockSpec(wuv.shape, lambda b, p, qi, kj: (0, 0, 0)),
        pl.BlockSpec(wo.shape, lambda b, p, qi, kj: (0, 0)),
    ]
    grid_spec = pltpu.PrefetchScalarGridSpec(
        num_scalar_prefetch=2, grid=(batch, len(pairs)), in_specs=in_specs,
        out_specs=pl.BlockSpec((tq, d), lambda b, p, qi, kj: (b * nq + qi[p], 0)),
        scratch_shapes=[pltpu.VMEM((h * tq, 1), F32), pltpu.VMEM((h * tq, 1), F32),
                        pltpu.VMEM((h * tq, kvr), F32), pltpu.VMEM((tq, wo.shape[0]), BF)])
    return pl.pallas_call(
        functools.partial(_flash_body, tq=tq, tk=tk, n_heads=h, hc=hc, c2=scale * LOG2E),
        grid_spec=grid_spec, out_shape=jax.ShapeDtypeStruct((batch * seq, d), F32),
        compiler_params=_cparams(("parallel", "arbitrary")), name="mla_flash_prompt",
    )(qi, kj, qlat, qpe, c_bf, kpe_bf, kst, x, wuv, wo)


def _decode_body(pt_ref, q_ref, qp_ref, *rest, pps, page, n_heads, n_new, kvr, rope, c2):
    kv_refs = rest[:pps]
    ks_refs = rest[pps:2 * pps]
    newkv_ref, newks_ref, o_ref, kbuf, kst_sc, m_sc, l_sc, acc_sc = rest[2 * pps:]
    cidx = pl.program_id(1)
    nc = pl.num_programs(1)
    rows = n_heads * n_new

    @pl.when(cidx == 0)
    def _():
        m_sc[...] = jnp.full(m_sc.shape, NEG_BIG, F32)
        l_sc[...] = jnp.zeros(l_sc.shape, F32)
        acc_sc[...] = jnp.zeros(acc_sc.shape, F32)

    q = q_ref[...]
    qp = qp_ref[:, :rope]

    def attend(k_all, kst, keep):
        c = k_all[:, :kvr]
        s_n = _dot_nt(q, c)
        s_p = _dot_nt(qp, k_all[:, kvr:kvr + rope])
        kst = kst * c2
        parts = []
        for h in range(n_heads):
            sl = slice(h * n_new, (h + 1) * n_new)
            parts.append(s_n[sl] * kst[h:h + 1, :] + s_p[sl] * c2)
        s = jnp.concatenate(parts, axis=0)
        if keep is not None:
            s = jnp.where(keep, s, NEG_BIG)
        m_prev = m_sc[...]
        m_cur = jnp.maximum(m_prev, jnp.max(s, axis=1, keepdims=True))
        alpha = jnp.exp2(m_prev - m_cur)
        p = jnp.exp2(s - m_cur)
        l_sc[...] = alpha * l_sc[...] + jnp.sum(p, axis=1, keepdims=True)
        acc_sc[...] = alpha * acc_sc[...] + _dot(p.astype(BF), c)
        m_sc[...] = m_cur

    for p in range(pps):
        kbuf[p * page:(p + 1) * page, :] = kv_refs[p][...].astype(BF)
        kst_sc[:, p * page:(p + 1) * page] = ks_refs[p][...].T
    attend(kbuf[...], kst_sc[...], None)

    @pl.when(cidx == nc - 1)
    def _():
        r = lax.broadcasted_iota(jnp.int32, (rows, page), 0)
        kj = lax.broadcasted_iota(jnp.int32, (rows, page), 1)
        keep = kj <= (r % n_new)
        attend(newkv_ref[...].astype(BF), newks_ref[...], keep)
        o_ref[...] = acc_sc[...] * (1.0 / l_sc[...])


def _decode_attend(page_table, qlat_s, qpe_s, cache_kv, j, cache_ks, newkv, newkst, *, pps, scale):
    bd, n_pages = page_table.shape
    rows, kvr = qlat_s.shape[1:]
    page, width = cache_kv.shape[2:]
    n_heads = cache_ks.shape[3]
    n_new = rows // n_heads
    rope = width - kvr
    nc = n_pages // pps
    pt_flat = page_table.reshape(-1)

    def page_map_kv(p):
        return lambda b, c, pt: (j, pt[b * n_pages + c * pps + p], 0, 0)

    in_specs = [pl.BlockSpec((None, rows, kvr), lambda b, c, pt: (b, 0, 0)),
                pl.BlockSpec((None, rows, LANES), lambda b, c, pt: (b, 0, 0))]
    in_specs += [pl.BlockSpec((None, None, page, width), page_map_kv(p)) for p in range(pps)]
    in_specs += [pl.BlockSpec((None, None, page, n_heads), page_map_kv(p)) for p in range(pps)]
    in_specs += [pl.BlockSpec((None, page, width), lambda b, c, pt: (b, 0, 0)),
                 pl.BlockSpec((None, n_heads, page), lambda b, c, pt: (b, 0, 0))]
    grid_spec = pltpu.PrefetchScalarGridSpec(
        num_scalar_prefetch=1, grid=(bd, nc), in_specs=in_specs,
        out_specs=pl.BlockSpec((None, rows, kvr), lambda b, c, pt: (b, 0, 0)),
        scratch_shapes=[pltpu.VMEM((pps * page, width), BF), pltpu.VMEM((n_heads, pps * page), F32),
                        pltpu.VMEM((rows, 1), F32), pltpu.VMEM((rows, 1), F32), pltpu.VMEM((rows, kvr), F32)])
    return pl.pallas_call(
        functools.partial(_decode_body, pps=pps, page=page, n_heads=n_heads, n_new=n_new, kvr=kvr,
                          rope=rope, c2=scale * LOG2E),
        grid_spec=grid_spec, out_shape=jax.ShapeDtypeStruct((bd, rows, kvr), F32),
        compiler_params=_cparams(("parallel", "arbitrary")), name="mla_decode",
    )(pt_flat, qlat_s, qpe_s, *([cache_kv] * pps), *([cache_ks] * pps), newkv, newkst)


def _ffn_dense_body(x_ref, g_ref, w1_ref, w3_ref, w2_ref, o_ref, h_sc):
    j = pl.program_id(1)

    @pl.when(j == 0)
    def _():
        x = x_ref[...]
        h_sc[...] = (_rms(x) * g_ref[...]).astype(BF)
        o_ref[...] = x

    h = h_sc[...]
    a = _dot(h, w1_ref[...])
    b = _dot(h, w3_ref[...])
    g = (a * _sigmoid(a)) * b
    o_ref[...] += _dot(g.astype(BF), w2_ref[...])


def _ffn_dense(x, g, w1, w3, w2, *, bm, bf):
    t, d = x.shape
    ff = w1.shape[1]
    return pl.pallas_call(
        _ffn_dense_body, grid=(t // bm, ff // bf),
        in_specs=[pl.BlockSpec((bm, d), lambda i, j: (i, 0)), _full(g.shape),
                  pl.BlockSpec((d, bf), lambda i, j: (0, j)), pl.BlockSpec((d, bf), lambda i, j: (0, j)),
                  pl.BlockSpec((bf, d), lambda i, j: (j, 0))],
        out_specs=pl.BlockSpec((bm, d), lambda i, j: (i, 0)),
        out_shape=jax.ShapeDtypeStruct((t, d), F32),
        scratch_shapes=[pltpu.VMEM((bm, d), BF)],
        compiler_params=_cparams(("parallel", "arbitrary")), name="ffn_dense",
    )(x, g, w1, w3, w2)


def _ffn_group_body(te_ref, nt_ref, x_ref, w1_ref, w3_ref, w2_ref, o_ref, h_sc):
    i = pl.program_id(0)
    j = pl.program_id(1)

    @pl.when(i < nt_ref[0])
    def _():
        @pl.when(j == 0)
        def _():
            h_sc[...] = x_ref[...].astype(BF)
            o_ref[...] = jnp.zeros(o_ref.shape, F32)

        h = h_sc[...]
        a = _dot(h, w1_ref[...])
        b = _dot(h, w3_ref[...])
        g = (a * _sigmoid(a)) * b
        o_ref[...] += _dot(g.astype(BF), w2_ref[...])


def _ffn_grouped(tile_expert, n_tiles_used, xs, w1, w3, w2, *, bm, bf):
    r, d = xs.shape
    ff = w1.shape[2]
    grid_spec = pltpu.PrefetchScalarGridSpec(
        num_scalar_prefetch=2, grid=(r // bm, ff // bf),
        in_specs=[pl.BlockSpec((bm, d), lambda i, j, te, nt: (i, 0)),
                  pl.BlockSpec((None, d, bf), lambda i, j, te, nt: (te[i], 0, j)),
                  pl.BlockSpec((None, d, bf), lambda i, j, te, nt: (te[i], 0, j)),
                  pl.BlockSpec((None, bf, d), lambda i, j, te, nt: (te[i], j, 0))],
        out_specs=pl.BlockSpec((bm, d), lambda i, j, te, nt: (i, 0)),
        scratch_shapes=[pltpu.VMEM((bm, d), BF)])
    return pl.pallas_call(
        _ffn_group_body, grid_spec=grid_spec, out_shape=jax.ShapeDtypeStruct((r, d), F32),
        compiler_params=_cparams(("parallel", "arbitrary")), name="ffn_grouped",
    )(tile_expert, n_tiles_used, xs, w1, w3, w2)


def _router_body(x_ref, g_ref, wr_ref, tri_ref, f_ref, info_ref, cnt_ref, carry_sc, *, n_experts):
    @pl.when(pl.program_id(0) == 0)
    def _():
        carry_sc[...] = jnp.zeros(carry_sc.shape, F32)

    f = _rms(x_ref[...]) * g_ref[...]
    f_ref[...] = f
    lane = lax.broadcasted_iota(jnp.int32, (f.shape[0], LANES), 1)
    f_hi = f.astype(BF)
    f_lo = (f - f_hi.astype(F32)).astype(BF)
    raw = _dot(f_hi, wr_ref[0]) + (_dot(f_lo, wr_ref[0]) + _dot(f_hi, wr_ref[1]))
    logits = jnp.where(lane < n_experts, raw, -jnp.inf)
    v1 = jnp.max(logits, axis=1, keepdims=True)
    i1 = jnp.min(jnp.where(logits == v1, lane, LANES), axis=1, keepdims=True)
    rest = jnp.where(lane == i1, -jnp.inf, logits)
    v2 = jnp.max(rest, axis=1, keepdims=True)
    i2 = jnp.min(jnp.where(rest == v2, lane, LANES), axis=1, keepdims=True)
    e2 = jnp.exp(v2 - v1)
    den = 1.0 + e2
    g1 = 1.0 / den
    g2 = e2 / den
    hot1 = lane == i1
    hot2 = lane == i2
    onehot = jnp.where(hot1 | hot2, 1.0, 0.0)
    rank = _dot(tri_ref[...], onehot.astype(BF)) + carry_sc[...]
    carry = carry_sc[...] + jnp.sum(onehot, axis=0, keepdims=True)
    carry_sc[...] = carry
    cnt_ref[...] = jnp.broadcast_to(carry, cnt_ref.shape)
    r1 = jnp.sum(jnp.where(hot1, rank, 0.0), axis=1, keepdims=True)
    r2 = jnp.sum(jnp.where(hot2, rank, 0.0), axis=1, keepdims=True)
    info = jnp.zeros((f.shape[0], LANES), F32)
    for k, col in enumerate((i1.astype(F32), i2.astype(F32), r1, r2, g1, g2)):
        info = jnp.where(lane == k, col, info)
    info_ref[...] = info


def _router(x, g, wr, *, bm):
    t, d = x.shape
    n_experts = wr.shape[1]
    wr_f = _pad_lanes(wr.astype(F32))
    wr_hi = wr_f.astype(BF)
    wr_p = jnp.stack([wr_hi, (wr_f - wr_hi.astype(F32)).astype(BF)])
    tri =(jnp.arange(bm)[:, None] > jnp.arange(bm)[None, :]).astype(BF)
    return pl.pallas_call(
        functools.partial(_router_body, n_experts=n_experts), grid=(t // bm,),
        in_specs=[pl.BlockSpec((bm, d), lambda i: (i, 0)), _full(g.shape), _full(wr_p.shape), _full(tri.shape)],
        out_specs=(pl.BlockSpec((bm, d), lambda i: (i, 0)), pl.BlockSpec((bm, LANES), lambda i: (i, 0)),
                   pl.BlockSpec((SUBLANES, LANES), lambda i: (0, 0))),
        out_shape=(jax.ShapeDtypeStruct((t, d), F32), jax.ShapeDtypeStruct((t, LANES), F32),
                   jax.ShapeDtypeStruct((SUBLANES, LANES), F32)),
        scratch_shapes=[pltpu.VMEM((1, LANES), F32)],
        compiler_params=_cparams(("arbitrary",)), name="moe_router",
    )(x, g, wr_p, tri)


SC_WORDS = 256
SC_WINDOW = 128


def _sc_mesh():
    return plsc.VectorSubcoreMesh(core_axis_name="c", subcore_axis_name="s")


def _sc_dispatch(pieces, idx_a, idx_b, n_out):
    n, w = pieces.shape

    @pl.kernel(out_type=jax.ShapeDtypeStruct((n_out, w), pieces.dtype), mesh=_sc_mesh())
    def k(x_hbm, ia_hbm, ib_hbm, o_hbm):
        def body(x_vmem, ia_vmem, ib_vmem):
            pltpu.sync_copy(x_vmem, o_hbm.at[ia_vmem.at[0]])
            pltpu.sync_copy(x_vmem, o_hbm.at[ib_vmem.at[0]])

        pltpu.emit_pipeline(
            body, grid=(n // SC_WINDOW,),
            in_specs=[pl.BlockSpec((SC_WINDOW, w), lambda i: (i, 0)),
                      pl.BlockSpec((1, SC_WINDOW), lambda i: (0, i)),
                      pl.BlockSpec((1, SC_WINDOW), lambda i: (0, i))],
            out_specs=[], core_axis_name=('c', 's'), dimension_semantics=(pltpu.PARALLEL,),
        )(x_hbm, ia_hbm, ib_hbm)

    return k(pieces, idx_a.reshape(1, n), idx_b.reshape(1, n))


def _sc_gather(table, idx):
    n = idx.shape[0]
    w = table.shape[1]

    @pl.kernel(out_type=jax.ShapeDtypeStruct((n, w), table.dtype), mesh=_sc_mesh())
    def k(t_hbm, i_hbm, o_hbm):
        def body(i_vmem, o_vmem):
            pltpu.sync_copy(t_hbm.at[i_vmem.at[0]], o_vmem)

        pltpu.emit_pipeline(
            body, grid=(n // SC_WINDOW,),
            in_specs=[pl.BlockSpec((1, SC_WINDOW), lambda i: (0, i))],
            out_specs=[pl.BlockSpec((SC_WINDOW, w), lambda i: (i, 0))],
            core_axis_name=('c', 's'), dimension_semantics=(pltpu.PARALLEL,),
        )(i_hbm, o_hbm)

    return k(table, idx.reshape(1, n))


def _combine_body(x_ref, ya_ref, yb_ref, info_ref, o_ref):
    info = info_ref[...]
    o_ref[...] = x_ref[...] + (info[:, 4:5] * ya_ref[...] + info[:, 5:6] * yb_ref[...])


def _moe_combine(x, ya, yb, info, *, bm):
    t, d = x.shape
    row = lambda i: (i, 0)
    return pl.pallas_call(
        _combine_body, grid=(t // bm,),
        in_specs=[pl.BlockSpec((bm, d), row), pl.BlockSpec((bm, d), row), pl.BlockSpec((bm, d), row),
                  pl.BlockSpec((bm, LANES), row)],
        out_specs=pl.BlockSpec((bm, d), row), out_shape=jax.ShapeDtypeStruct((t, d), F32),
        compiler_params=_cparams(("parallel",)), name="moe_combine",
    )(x, ya, yb, info)


def _gmlp_body(x_ref, g_ref, win_ref, gv_ref, ws_ref, bs_ref, wout_ref, y_ref, *maybe_v, chunk, groups):
    x = x_ref[...]
    bm = x.shape[0]
    n = (_rms(x) * g_ref[...]).astype(BF)
    z = _gelu(_dot(n, win_ref[...]))
    e = z.shape[1] // 2
    dg = e // groups
    u = z[:, :e]
    v = _rms(z[:, e:]) * gv_ref[...]
    if maybe_v:
        maybe_v[0][...] = v
    vb = v.astype(BF)
    bs = bs_ref[...]
    rows = []
    for c in range(bm // chunk):
        cols = []
        for g in range(groups):
            mixed = _dot(ws_ref[g], vb[c * chunk:(c + 1) * chunk, g * dg:(g + 1) * dg]) + bs[:, g:g + 1]
            cols.append(u[c * chunk:(c + 1) * chunk, g * dg:(g + 1) * dg] * mixed)
        rows.append(jnp.concatenate(cols, axis=1))
    y = jnp.concatenate(rows, axis=0) if len(rows) > 1 else rows[0]
    y_ref[...] = x + _dot(y.astype(BF), wout_ref[...])


def _gmlp(x, g, win, gv, ws, bs, wout, *, bm, emit_v):
    t, d = x.shape
    e = wout.shape[0]
    groups, chunk, _ = ws.shape
    out_shape = [jax.ShapeDtypeStruct((t, d), F32)]
    out_specs = [pl.BlockSpec((bm, d), lambda i: (i, 0))]
    if emit_v:
        out_shape.append(jax.ShapeDtypeStruct((t, e), F32))
        out_specs.append(pl.BlockSpec((bm, e), lambda i: (i, 0)))
    res = pl.pallas_call(
        functools.partial(_gmlp_body, chunk=chunk, groups=groups),
        grid=(t // bm,),
        in_specs=[pl.BlockSpec((bm, d), lambda i: (i, 0)), _full(g.shape), _full(win.shape), _full(gv.shape),
                  _full(ws.shape), _full(bs.shape), _full(wout.shape)],
        out_specs=tuple(out_specs), out_shape=tuple(out_shape),
        compiler_params=_cparams(("parallel",)), name="gmlp_v" if emit_v else "gmlp",
    )(x, g, win, gv, ws, bs, wout)
    return res


def _scan8(a8, u8):
    row = lax.broadcasted_iota(jnp.int32, a8.shape, 0)
    A, B = a8, u8
    for d in (1, 2, 4):
        a_sh = pltpu.roll(A, d, 0)
        b_sh = pltpu.roll(B, d, 0)
        m = row >= d
        B = jnp.where(m, A * b_sh + B, B)
        A = jnp.where(m, A * a_sh, A)
    return A, B


def _rglru_body(x_ref, g_ref, h0_ref, buf_ref, wgate_ref, wx_ref, cw_ref, cb_ref, wa_ref, ba_ref,
                wi_ref, bi_ref, sp_ref, wout_ref, y_ref, hl_ref, xl_ref,
                xe_sc, a_sc, u_sc, hs_sc, carry_sc, *, segmented, gw):
    bt = x_ref.shape[0]
    ec = wx_ref.shape[1]
    x = x_ref[...]
    n = (_rms(x) * g_ref[...]).astype(BF)
    gate = _gelu(_dot(n, wgate_ref[...]))
    xb = _dot(n, wx_ref[...])
    cw = cw_ref[...]
    cb = cb_ref[...]
    ngr = bt // SUBLANES

    if segmented:
        xe_sc[SUBLANES:, :] = xb
        xl_ref[...] = xb

        def conv_group(gi, _):
            base = pl.multiple_of(gi * SUBLANES, SUBLANES)
            cur = xe_sc[pl.ds(SUBLANES + base, SUBLANES), :]
            hist = buf_ref[gi]
            a_sc[0:SUBLANES, :] = hist
            a_sc[SUBLANES:2 * SUBLANES, :] = cur
            acc = cb + cw[3:4, :] * cur
            for dly in (1, 2, 3):
                acc = acc + cw[3 - dly:4 - dly, :] * a_sc[pl.ds(SUBLANES - dly, SUBLANES), :]
            u_sc[pl.ds(base, SUBLANES), :] = acc
            return 0

        lax.fori_loop(0, ngr, conv_group, 0)
        xc = u_sc[...]
    else:
        t_idx = pl.program_id(1)

        @pl.when(t_idx == 0)
        def _():
            xe_sc[0:SUBLANES, :] = jnp.zeros((SUBLANES, ec), F32)
            carry_sc[...] = jnp.zeros(carry_sc.shape, F32)

        xe_sc[SUBLANES:, :] = xb
        xc = cb + cw[3:4, :] * xb
        for dly in (1, 2, 3):
            xc = xc + cw[3 - dly:4 - dly, :] * xe_sc[pl.ds(SUBLANES - dly, bt), :]
        xl_ref[...] = xb[bt - SUBLANES:, :]
        xe_sc[0:SUBLANES, :] = xb[bt - SUBLANES:, :]

    xcb = xc.astype(BF)
    ra, ri = [], []
    for k in range(ec // gw):
        blk = xcb[:, k * gw:(k + 1) * gw]
        ra.append(_dot(blk, wa_ref[k]))
        ri.append(_dot(blk, wi_ref[k]))
    r = _sigmoid(jnp.concatenate(ra, axis=1) + ba_ref[...])
    ig = _sigmoid(jnp.concatenate(ri, axis=1) + bi_ref[...])
    log_a = (-C_RG) * r * sp_ref[...]
    a = jnp.exp(log_a)
    mult = jnp.sqrt(-jnp.tanh(log_a) * (a * a + 1.0))
    a_sc[0:bt, :] = a
    u_sc[...] = mult * ig * xc

    if segmented:
        def scan_group(gi, _):
            base = pl.multiple_of(gi * SUBLANES, SUBLANES)
            A, B = _scan8(a_sc[pl.ds(base, SUBLANES), :], u_sc[pl.ds(base, SUBLANES), :])
            h8 = A * h0_ref[pl.ds(gi, 1), :] + B
            hs_sc[pl.ds(base, SUBLANES), :] = h8
            hl_ref[pl.ds(gi, 1), :] = h8[SUBLANES - 1:SUBLANES, :]
            return 0

        lax.fori_loop(0, ngr, scan_group, 0)
    else:
        def scan_group(gi, carry):
            base = pl.multiple_of(gi * SUBLANES, SUBLANES)
            A, B = _scan8(a_sc[pl.ds(base, SUBLANES), :], u_sc[pl.ds(base, SUBLANES), :])
            h8 = A * carry + B
            hs_sc[pl.ds(base, SUBLANES), :] = h8
            return h8[SUBLANES - 1:SUBLANES, :]

        last = lax.fori_loop(0, ngr, scan_group, carry_sc[...])
        carry_sc[...] = last
        hl_ref[...] = last

    y_ref[...] = x + _dot((hs_sc[...] * gate).astype(BF), wout_ref[...])


def _rglru(x, g, h0, buf8, w, *, bt, batch, seq, segmented):
    d = x.shape[1]
    ec = w['wx'].shape[1]
    gw = w['wa'].shape[1]
    names = ['wgate', 'wx', 'cw', 'cb', 'wa', 'ba', 'wi', 'bi', 'sp', 'wout']
    wspecs = [_full(w[k].shape) for k in names]
    if segmented:
        t = x.shape[0]
        nseq = t // SUBLANES
        sb = bt // SUBLANES
        grid = (t // bt,)
        in_specs = [pl.BlockSpec((bt, d), lambda i: (i, 0)), _full(g.shape),
                    pl.BlockSpec((sb, ec), lambda i: (i, 0)),
                    pl.BlockSpec((sb, SUBLANES, ec), lambda i: (i, 0, 0))] + wspecs
        out_shape = (jax.ShapeDtypeStruct((t, d), F32), jax.ShapeDtypeStruct((nseq, ec), F32),
                     jax.ShapeDtypeStruct((t, ec), F32))
        out_specs = (pl.BlockSpec((bt, d), lambda i: (i, 0)), pl.BlockSpec((sb, ec), lambda i: (i, 0)),
                     pl.BlockSpec((bt, ec), lambda i: (i, 0)))
        sem = ("parallel",)
    else:
        nt = seq // bt
        grid = (batch, nt)
        in_specs = [pl.BlockSpec((bt, d), lambda b, i: (b * nt + i, 0)), _full(g.shape), _full(h0.shape),
                    _full(buf8.shape)] + wspecs
        out_shape = (jax.ShapeDtypeStruct((batch * seq, d), F32), jax.ShapeDtypeStruct((batch, 1, ec), F32),
                     jax.ShapeDtypeStruct((batch, SUBLANES, ec), F32))
        out_specs = (pl.BlockSpec((bt, d), lambda b, i: (b * nt + i, 0)),
                     pl.BlockSpec((None, 1, ec), lambda b, i: (b, 0, 0)),
                     pl.BlockSpec((None, SUBLANES, ec), lambda b, i: (b, 0, 0)))
        sem = ("parallel", "arbitrary")
    scratch = [pltpu.VMEM((bt + SUBLANES, ec), F32), pltpu.VMEM((max(bt, 2 * SUBLANES), ec), F32),
               pltpu.VMEM((bt, ec), F32), pltpu.VMEM((bt, ec), F32), pltpu.VMEM((1, ec), F32)]
    return pl.pallas_call(
        functools.partial(_rglru_body, segmented=segmented, gw=gw),
        grid=grid, in_specs=in_specs, out_specs=out_specs, out_shape=out_shape, scratch_shapes=scratch,
        compiler_params=_cparams(sem), name="rglru_seg" if segmented else "rglru",
    )(x, g, h0, buf8, *[w[k] for k in names])


def _row(v):
    return v.reshape(1, -1).astype(F32)


def _pad_lanes(v, width=LANES):
    return jnp.pad(v, [(0, 0)] * (v.ndim - 1) + [(0, width - v.shape[-1])])


def _prep_mla(g_mix_l, wq_a, g_qa, wq_b, wkv_a, g_kva, w_uk, w_uv, wo, g_qn, g_qr, g_kn, g_kr):
    qr, h, dq = wq_b.shape
    kvr = w_uk.shape[0]
    nope = w_uk.shape[2]
    rope = dq - nope
    w = {
        'gmix': _row(g_mix_l),
        'wqa': wq_a.astype(BF), 'gqa': _row(g_qa),
        'wqn': wq_b[:, :, :nope].reshape(qr, h * nope).astype(BF),
        'wqr': _pad_lanes(wq_b[:, :, nope:]).reshape(qr, h * LANES).astype(BF),
        'wkvc': wkv_a[:, :kvr].astype(BF),
        'wkvr': _pad_lanes(wkv_a[:, kvr:]).astype(BF),
        'gkva': _row(g_kva), 'gkr': _pad_lanes(_row(g_kr)),
        'wukf': w_uk.reshape(kvr, h * nope).astype(BF),
        'wukt': jnp.transpose(w_uk, (1, 2, 0)).astype(BF),
        'gqn': _row(g_qn), 'gqr': _pad_lanes(_row(g_qr)), 'gkn': _row(g_kn),
    }
    wuv = jnp.transpose(w_uv, (1, 0, 2)).astype(BF)
    wo2 = wo.reshape(-1, wo.shape[-1]).astype(BF)
    return w, wuv, wo2, rope


def _rope_tables(pos, rope):
    half = rope // 2
    inv_freq = ROPE_THETA ** (-jnp.arange(half, dtype=F32) / half)
    ang = pos.astype(F32)[:, None] * inv_freq[None, :]
    cos, sin = jnp.cos(ang), jnp.sin(ang)
    cos_t = _pad_lanes(jnp.concatenate([cos, cos], axis=1))
    sin_t = _pad_lanes(jnp.concatenate([-sin, sin], axis=1))
    return cos_t, sin_t


def _mla_layer(x, j, cos_t, sin_t, prm, cache_kv, cache_ks, page_table, dims):
    batch, seq, bd, ns = dims
    tp = batch * seq
    w, wuv, wo2, rope = prm
    h, kvr, _ = wuv.shape
    scale = float(w['wukt'].shape[1] + rope) ** -0.5
    qlat, qpe, ckv, kpe, ks = _mla_proj(x, cos_t, sin_t, w, bm=512)
    kv_row = jnp.concatenate([ckv, kpe[:, :rope]], axis=1)
    ks8 = ks[:, :h]
    c_bf = ckv[:tp].astype(BF)
    kpe_bf = kpe[:tp].astype(BF)
    kst = jnp.transpose(ks8[:tp])
    y_p = _flash_prompt(qlat, qpe, c_bf, kpe_bf, kst, x, wuv, wo2, batch=batch, seq=seq, tq=FLASH_TQ,
                        tk=FLASH_TK, hc=FLASH_HC, scale=scale)
    page = cache_kv.shape[2]
    qlat_s = jnp.transpose(qlat[:, tp:].reshape(h, bd, ns, kvr), (1, 0, 2, 3)).reshape(bd, h * ns, kvr)
    qpe_s = jnp.transpose(qpe[:, tp:].reshape(h, bd, ns, LANES), (1, 0, 2, 3)).reshape(bd, h * ns, LANES)
    newkv = jnp.pad(kv_row[tp:].reshape(bd, ns, kvr + rope), ((0, 0), (0, page - ns), (0, 0)))
    newkst = jnp.pad(jnp.transpose(ks8[tp:].reshape(bd, ns, h), (0, 2, 1)), ((0, 0), (0, 0), (0, page - ns)))
    o_s = _decode_attend(page_table, qlat_s, qpe_s, cache_kv, j, cache_ks, newkv, newkst, pps=DECODE_PPS,
                         scale=scale)
    o_s = jnp.transpose(o_s.reshape(bd, h, ns, kvr), (0, 2, 1, 3)).reshape(bd * ns, h * kvr)
    y_s = _mla_out(o_s, x[tp:], wuv, wo2, bm=min(512, bd * ns))
    y = jnp.concatenate([y_p, y_s], axis=0)
    return y, kv_row, ks8


def _gmlp_layer(x, g_mix_l, w_in, g_v, w_s, b_s, w_out, dims):
    batch, seq, bd, ns = dims
    tp = batch * seq
    groups, chunk, _ = w_s.shape
    tri = jnp.tril(jnp.ones((chunk, chunk), bool))
    ws_p = jnp.where(tri[None], w_s, 0).astype(BF)
    bs_p = _pad_lanes(jnp.transpose(b_s))
    l = min(ns, chunk)
    tri_s = jnp.tril(jnp.ones((l, l), bool))
    ws_small = jnp.where(tri_s[None], w_s[:, :l, :l], 0)
    eye = jnp.eye(chunk // l, dtype=F32)
    ws_s = jnp.einsum('ab,gts->gatbs', eye, ws_small).reshape(groups, chunk, chunk).astype(BF)
    bs_s = _pad_lanes(jnp.tile(jnp.transpose(b_s[:, :l]), (chunk // l, 1)))
    args = (_row(g_mix_l), w_in.astype(BF), _row(g_v))
    wout = w_out.astype(BF)
    (y_p,) = _gmlp(x[:tp], *args, ws_p, bs_p, wout, bm=256, emit_v=False)
    y_s, v_s = _gmlp(x[tp:], *args, ws_s, bs_s, wout, bm=256, emit_v=True)
    return jnp.concatenate([y_p, y_s], axis=0), v_s


def _rglru_layer(x, g_mix_l, h0_s, buf_s, w_gate, w_x, conv_w, conv_b, w_a, b_a, w_i, b_i, lam, w_out, dims):
    batch, seq, bd, ns = dims
    tp = batch * seq
    nb, db, _ = w_a.shape
    ec = nb * db
    pair = 2
    gw = db * pair

    def blockdiag(wb):
        wb = wb.reshape(nb // pair, pair, db, db)
        eye = jnp.eye(pair, dtype=wb.dtype)
        return jnp.einsum('kpde,pq->kpdqe', wb, eye).reshape(nb // pair, gw, gw).astype(BF)

    w = {'wgate': w_gate.astype(BF), 'wx': w_x.astype(BF), 'cw': _pad_rows(conv_w), 'cb': _row(conv_b),
         'wa': blockdiag(w_a), 'ba': _row(b_a), 'wi': blockdiag(w_i), 'bi': _row(b_i),
         'sp': _row(jax.nn.softplus(-lam.astype(F32))), 'wout': w_out.astype(BF)}
    g = _row(g_mix_l)
    cw = conv_w.shape[0]
    zeros_h = jnp.zeros((SUBLANES, ec), F32)
    zeros_b = jnp.zeros((1, SUBLANES, ec), F32)
    y_p, hl_p, xl_p = _rglru(x[:tp], g, zeros_h, zeros_b, w, bt=256, batch=batch, seq=seq, segmented=False)
    buf8 = jnp.pad(buf_s.astype(F32), ((0, 0), (SUBLANES - (cw - 1), 0), (0, 0)))
    y_s, hl_s, xl_s = _rglru(x[tp:], g, h0_s.astype(F32), buf8, w, bt=256, batch=bd, seq=ns, segmented=True)
    h_p = hl_p[:, 0, :]
    conv_p = xl_p[:, SUBLANES - (cw - 1):, :]
    conv_s = xl_s.reshape(bd, ns, ec)[:, ns - (cw - 1):, :]
    return jnp.concatenate([y_p, y_s], axis=0), h_p, conv_p, hl_s, conv_s


def _pad_rows(v, rows=SUBLANES):
    return jnp.pad(v.astype(F32), ((0, rows - v.shape[0]), (0, 0)))


def _moe_layer(x, g_ffn_l, router, w1, w3, w2, *, bm=512, bf=512):
    t, d = x.shape
    ne = router.shape[1]
    f, info, cnt = _router(x, _row(g_ffn_l), router, bm=512)
    counts = cnt[0, :ne].astype(jnp.int32)
    tiles_per = (counts + bm - 1) // bm
    tile_start = jnp.cumsum(tiles_per) - tiles_per
    n_tiles = (t * TOP_K) // bm + ne
    r_pad = n_tiles * bm
    tile_ids = jnp.arange(n_tiles)
    n_used = jnp.sum(tiles_per)
    tile_expert = jnp.sum((tile_ids[:, None] >= tile_start[None, :]).astype(jnp.int32), axis=1) - 1
    last_e = jnp.max(jnp.where(tiles_per > 0, jnp.arange(ne), 0))
    tile_expert = jnp.where(tile_ids < n_used, tile_expert, last_e).astype(jnp.int32)
    base = (tile_start * bm).astype(jnp.int32)
    sel = info[:, :TOP_K].astype(jnp.int32)
    dest = jnp.take(base, sel) + info[:, TOP_K:2 * TOP_K].astype(jnp.int32)
    npc = d // SC_WORDS
    piece_idx = (dest[:, None, :] * npc + jnp.arange(npc, dtype=jnp.int32)[None, :, None]).reshape(t * npc, TOP_K)
    idx_a, idx_b = piece_idx[:, 0], piece_idx[:, 1]
    xs = _sc_dispatch(f.reshape(t * npc, SC_WORDS), idx_a, idx_b, r_pad * npc).reshape(r_pad, d)
    ys = _ffn_grouped(tile_expert, n_used.reshape(1).astype(jnp.int32), xs,
                      w1.astype(BF), w3.astype(BF), w2.astype(BF), bm=bm, bf=bf)
    ys_p = ys.reshape(r_pad * npc, SC_WORDS)
    ya = _sc_gather(ys_p, idx_a).reshape(t, d)
    yb = _sc_gather(ys_p, idx_b).reshape(t, d)
    return _moe_combine(x, ya, yb, info, bm=512)


def kernel(x_prompt, x_sample, cache_mla_kv, cache_mla_kscale, state_rglru_h, state_rglru_conv, page_table, g_mix, g_ffn, mla_wq_a, mla_g_qa, mla_wq_b, mla_wkv_a, mla_g_kva, mla_w_uk, mla_w_uv, mla_wo, mla_g_qn, mla_g_qr, mla_g_kn, mla_g_kr, gm_w_in, gm_g_v, gm_w_s, gm_b_s, gm_w_out, rg_w_gate, rg_w_x, rg_conv_w, rg_conv_b, rg_w_a, rg_b_a, rg_w_i, rg_b_i, rg_lam, rg_w_out, ffd_w1, ffd_w3, ffd_w2, moe_router, moe_w1, moe_w3, moe_w2):
    batch, seq, d = x_prompt.shape
    bd, ns, _ = x_sample.shape
    dims = (batch, seq, bd, ns)
    tp = batch * seq
    depth = g_mix.shape[0]
    n_mixers = 3
    past_len = page_table.shape[1] * cache_mla_kv.shape[2]

    x = jnp.concatenate([x_prompt.reshape(tp, d), x_sample.reshape(bd * ns, d)], axis=0)
    rope = mla_wq_b.shape[-1] - mla_w_uk.shape[-1]
    pos = jnp.concatenate([jnp.tile(jnp.arange(seq, dtype=jnp.int32), batch),
                           jnp.tile(past_len + jnp.arange(ns, dtype=jnp.int32), bd)])
    cos_t, sin_t = _rope_tables(pos, rope)

    kv_out, ks_out, v_out, hp_out, cp_out, hs_out, cs_out = [], [], [], [], [], [], []
    counts = [0, 0, 0]
    for layer in range(depth):
        kind = layer % n_mixers
        j = counts[kind]
        counts[kind] += 1
        if kind == 0:
            prm = _prep_mla(g_mix[layer], mla_wq_a[j], mla_g_qa[j], mla_wq_b[j], mla_wkv_a[j], mla_g_kva[j],
                            mla_w_uk[j], mla_w_uv[j], mla_wo[j], mla_g_qn[j], mla_g_qr[j], mla_g_kn[j],
                            mla_g_kr[j])
            x, kv_row, ks8 = _mla_layer(x, j, cos_t, sin_t, prm, cache_mla_kv, cache_mla_kscale, page_table, dims)
            kv_out.append(kv_row)
            ks_out.append(ks8)
        elif kind == 1:
            x, v_s = _gmlp_layer(x, g_mix[layer], gm_w_in[j], gm_g_v[j], gm_w_s[j], gm_b_s[j], gm_w_out[j], dims)
            v_out.append(v_s.reshape(bd, ns, -1))
        else:
            x, h_p, c_p, h_s, c_s = _rglru_layer(
                x, g_mix[layer], state_rglru_h[j], state_rglru_conv[j], rg_w_gate[j], rg_w_x[j], rg_conv_w[j],
                rg_conv_b[j], rg_w_a[j], rg_b_a[j], rg_w_i[j], rg_b_i[j], rg_lam[j], rg_w_out[j], dims)
            hp_out.append(h_p)
            cp_out.append(c_p)
            hs_out.append(h_s)
            cs_out.append(c_s)
        i = layer // 2
        if layer % 2 == 0:
            x = _ffn_dense(x, _row(g_ffn[layer]), ffd_w1[i].astype(BF), ffd_w3[i].astype(BF),
                           ffd_w2[i].astype(BF), bm=512, bf=1408)
        else:
            x = _moe_layer(x, g_ffn[layer], moe_router[i], moe_w1[i], moe_w3[i], moe_w2[i])

    kv_all = jnp.stack(kv_out)
    ks_all = jnp.stack(ks_out)
    return (x[:tp].reshape(batch, seq, d), x[tp:].reshape(bd, ns, d),
            kv_all[:, :tp].reshape(len(kv_out), batch, seq, -1), ks_all[:, :tp].reshape(len(ks_out), batch, seq, -1),
            kv_all[:, tp:].reshape(len(kv_out), bd, ns, -1), ks_all[:, tp:].reshape(len(ks_out), bd, ns, -1),
            jnp.stack(v_out), jnp.stack(hp_out), jnp.stack(cp_out), jnp.stack(hs_out), jnp.stack(cs_out))
```

```python
import functools

import jax
import jax.numpy as jnp
from jax import lax
from jax.experimental import pallas as pl
from jax.experimental.pallas import tpu as pltpu
from jax.experimental.pallas import tpu_sc as plsc

BF = jnp.bfloat16
F32 = jnp.float32
EPS = 1e-6
ROPE_THETA = 10000.0
C_RG = 8.0
TOP_K = 2
NEG_BIG = -1e30
LOG2E = 1.4426950408889634

LANES = 128
SUBLANES = 8
VMEM_LIMIT = 56 * 1024 * 1024

FLASH_TQ = 128
FLASH_TK = 1024
FLASH_HC = 2
DECODE_PPS = 32


def _cparams(sem):
    return pltpu.CompilerParams(dimension_semantics=sem, vmem_limit_bytes=VMEM_LIMIT)


def _dot(a, b):
    return jnp.dot(a, b, preferred_element_type=F32)


def _dot_nt(a, b):
    return lax.dot_general(a, b, (((1,), (1,)), ((), ())), preferred_element_type=F32)


def _rms(x):
    return x * lax.rsqrt(jnp.mean(x * x, axis=-1, keepdims=True) + EPS)


def _rms_half(x):
    return x * lax.rsqrt(jnp.sum(x * x, axis=-1, keepdims=True) * (1.0 / 64.0) + EPS)


def _rope_half(x, cos, sin):
    lane = lax.broadcasted_iota(jnp.int32, x.shape, 1)
    rot = jnp.where(lane < 32, pltpu.roll(x, 96, 1), pltpu.roll(x, 32, 1))
    return x * cos + rot * sin


def _gelu(x):
    return 0.5 * x * (1.0 + jnp.tanh(0.7978845608028654 * (x + 0.044715 * (x * x * x))))


def _sigmoid(x):
    return 1.0 / (1.0 + jnp.exp(-x))


def _full(shape):
    nd = len(shape)
    return pl.BlockSpec(shape, lambda *_: (0,) * nd)


def _mla_proj_body(x_ref, cos_ref, sin_ref, gmix_ref, wqa_ref, gqa_ref, wqn_ref, wqr_ref,
                   wkvc_ref, wkvr_ref, gkva_ref, gkr_ref, wukf_ref, wukt_ref, gqn_ref, gqr_ref,
                   gkn_ref, qlat_ref, qpe_ref, ckv_ref, kpe_ref, ks_ref, *, n_heads):
    x = x_ref[...]
    n = (_rms(x) * gmix_ref[...]).astype(BF)
    cos = cos_ref[...]
    sin = sin_ref[...]
    cq = (_rms(_dot(n, wqa_ref[...])) * gqa_ref[...]).astype(BF)
    ckv = _rms(_dot(n, wkvc_ref[...])) * gkva_ref[...]
    ckv_ref[...] = ckv
    kr = _dot(n, wkvr_ref[...])
    kpe_ref[...] = _rope_half(_rms_half(kr) * gkr_ref[...], cos, sin)
    kn = _dot(ckv.astype(BF), wukf_ref[...])
    lane = lax.broadcasted_iota(jnp.int32, (x.shape[0], LANES), 1)
    ks = jnp.zeros((x.shape[0], LANES), F32)
    for h in range(n_heads):
        blk = kn[:, h * 128:(h + 1) * 128]
        ksh = lax.rsqrt(jnp.mean(blk * blk, axis=-1, keepdims=True) + EPS)
        ks = jnp.where(lane == h, ksh, ks)
    ks_ref[...] = ks
    qn = _dot(cq, wqn_ref[...])
    qr = _dot(cq, wqr_ref[...])
    gq = gqn_ref[...]
    gk = gkn_ref[...]
    for h in range(n_heads):
        a = (_rms(qn[:, h * 128:(h + 1) * 128]) * gq) * gk
        qlat_ref[h] = _dot(a.astype(BF), wukt_ref[h]).astype(BF)
        r = _rms_half(qr[:, h * 128:(h + 1) * 128]) * gqr_ref[...]
        qpe_ref[h] = _rope_half(r, cos, sin).astype(BF)


def _mla_proj(x, cos, sin, w, *, bm):
    t, d = x.shape
    h = w['wukt'].shape[0]
    kvr = w['wkvc'].shape[1]
    row = lambda i: (i, 0)
    in_specs = [pl.BlockSpec((bm, d), row), pl.BlockSpec((bm, LANES), row), pl.BlockSpec((bm, LANES), row)]
    names = ['gmix', 'wqa', 'gqa', 'wqn', 'wqr', 'wkvc', 'wkvr', 'gkva', 'gkr', 'wukf', 'wukt', 'gqn', 'gqr', 'gkn']
    in_specs += [_full(w[k].shape) for k in names]
    out_shape = (jax.ShapeDtypeStruct((h, t, kvr), BF), jax.ShapeDtypeStruct((h, t, LANES), BF),
                 jax.ShapeDtypeStruct((t, kvr), F32), jax.ShapeDtypeStruct((t, LANES), F32),
                 jax.ShapeDtypeStruct((t, LANES), F32))
    out_specs = (pl.BlockSpec((h, bm, kvr), lambda i: (0, i, 0)), pl.BlockSpec((h, bm, LANES), lambda i: (0, i, 0)),
                 pl.BlockSpec((bm, kvr), row), pl.BlockSpec((bm, LANES), row), pl.BlockSpec((bm, LANES), row))
    return pl.pallas_call(
        functools.partial(_mla_proj_body, n_heads=h),
        grid=(t // bm,), in_specs=in_specs, out_specs=out_specs, out_shape=out_shape,
        compiler_params=_cparams(("parallel",)), name="mla_proj",
    )(x, cos, sin, *[w[k] for k in names])


def _mla_out_rows(o_heads, wuv_ref, wo_ref, obuf_ref):
    for h, oh in enumerate(o_heads):
        v = _dot(oh.astype(BF), wuv_ref[h])
        obuf_ref[:, h * 128:(h + 1) * 128] = v.astype(BF)
    return _dot(obuf_ref[...], wo_ref[...])


def _mla_out_body(o_ref, x_ref, wuv_ref, wo_ref, y_ref, obuf_ref, *, n_heads, kvr):
    o = o_ref[...]
    heads = [o[:, h * kvr:(h + 1) * kvr] for h in range(n_heads)]
    y_ref[...] = x_ref[...] + _mla_out_rows(heads, wuv_ref, wo_ref, obuf_ref)


def _mla_out(o, x, wuv, wo, *, bm):
    t, d = x.shape
    h, kvr, vd = wuv.shape
    return pl.pallas_call(
        functools.partial(_mla_out_body, n_heads=h, kvr=kvr),
        grid=(t // bm,),
        in_specs=[pl.BlockSpec((bm, h * kvr), lambda i: (i, 0)), pl.BlockSpec((bm, d), lambda i: (i, 0)),
                  _full(wuv.shape), _full(wo.shape)],
        out_specs=pl.BlockSpec((bm, d), lambda i: (i, 0)),
        out_shape=jax.ShapeDtypeStruct((t, d), F32),
        scratch_shapes=[pltpu.VMEM((bm, h * vd), BF)],
        compiler_params=_cparams(("parallel",)), name="mla_out",
    )(o, x, wuv, wo)


def _flash_body(qi_ref, kj_ref, qlat_ref, qpe_ref, c_ref, kpe_ref, kst_ref, x_ref, wuv_ref, wo_ref, y_ref,
                m_sc, l_sc, acc_sc, obuf_ref, *, tq, tk, n_heads, hc, c2):
    pidx = pl.program_id(1)
    i = qi_ref[pidx]
    j = kj_ref[pidx]
    last_needed = (i * tq + tq - 1) // tk
    kvr = qlat_ref.shape[-1]

    @pl.when(j == 0)
    def _():
        m_sc[...] = jnp.full(m_sc.shape, NEG_BIG, F32)
        l_sc[...] = jnp.zeros(l_sc.shape, F32)
        acc_sc[...] = jnp.zeros(acc_sc.shape, F32)

    def step(masked):
        c = c_ref[...]
        kpe = kpe_ref[...]
        kst = kst_ref[...] * c2
        if masked:
            qpos = i * tq + lax.broadcasted_iota(jnp.int32, (tq, tk), 0)
            kpos = j * tk + lax.broadcasted_iota(jnp.int32, (tq, tk), 1)
            keep = kpos <= qpos
        for g in range(n_heads // hc):
            r0, nr = g * hc * tq, hc * tq
            q = qlat_ref[g * hc:(g + 1) * hc].reshape(nr, kvr)
            qp = qpe_ref[g * hc:(g + 1) * hc].reshape(nr, LANES)
            s_n = _dot_nt(q, c)
            s_p = _dot_nt(qp, kpe)
            p_rows = []
            for hh in range(hc):
                h = g * hc + hh
                hr = slice(hh * tq, (hh + 1) * tq)
                pieces = []
                mx = None
                for kt in range(tk // LANES):
                    cols = slice(kt * LANES, (kt + 1) * LANES)
                    sh = s_n[hr, cols] * kst[h:h + 1, cols] + s_p[hr, cols] * c2
                    if masked:
                        sh = jnp.where(keep[:, cols], sh, NEG_BIG)
                    pieces.append(sh)
                    mx = sh if mx is None else jnp.maximum(mx, sh)
                hs = slice(h * tq, (h + 1) * tq)
                m_prev = m_sc[hs]
                m_cur = jnp.maximum(m_prev, jnp.max(mx, axis=1, keepdims=True))
                alpha = jnp.exp2(m_prev - m_cur)
                ps = []
                sm = None
                for sh in pieces:
                    p = jnp.exp2(sh - m_cur)
                    sm = p if sm is None else sm + p
                    ps.append(p.astype(BF))
                l_sc[hs] = alpha * l_sc[hs] + jnp.sum(sm, axis=1, keepdims=True)
                m_sc[hs] = m_cur
                acc_sc[hs] = alpha * acc_sc[hs]
                p_rows.append(jnp.concatenate(ps, axis=1))
            p_all = jnp.concatenate(p_rows, axis=0) if hc > 1 else p_rows[0]
            acc_sc[r0:r0 + nr] += _dot(p_all, c)

    @pl.when(j < last_needed)
    def _():
        step(False)

    @pl.when(j == last_needed)
    def _():
        step(True)
        inv = 1.0 / l_sc[...]
        heads = [acc_sc[h * tq:(h + 1) * tq, :] * inv[h * tq:(h + 1) * tq] for h in range(n_heads)]
        y_ref[...] = x_ref[...] + _mla_out_rows(heads, wuv_ref, wo_ref, obuf_ref)


def _flash_prompt(qlat, qpe, c_bf, kpe_bf, kst, x, wuv, wo, *, batch, seq, tq, tk, hc, scale):
    h, _, kvr = qlat.shape
    d = x.shape[1]
    nq = seq // tq
    nk = seq // tk
    pairs = [(i, j) for i in range(nq) for j in range((i * tq + tq - 1) // tk + 1)]
    qi = jnp.asarray([p[0] for p in pairs], jnp.int32)
    kj = jnp.asarray([p[1] for p in pairs], jnp.int32)
    in_specs = [
        pl.BlockSpec((h, tq, kvr), lambda b, p, qi, kj: (0, b * nq + qi[p], 0)),
        pl.BlockSpec((h, tq, LANES), lambda b, p, qi, kj: (0, b * nq + qi[p], 0)),
        pl.BlockSpec((tk, kvr), lambda b, p, qi, kj: (b * nk + kj[p], 0)),
        pl.BlockSpec((tk, LANES), lambda b, p, qi, kj: (b * nk + kj[p], 0)),
        pl.BlockSpec((h, tk), lambda b, p, qi, kj: (0, b * nk + kj[p])),
        pl.BlockSpec((tq, d), lambda b, p, qi, kj: (b * nq + qi[p], 0)),
        pl.BlockSpec(wuv.shape, lambda b, p, qi, kj: (0, 0, 0)),
        pl.BlockSpec(wo.shape, lambda b, p, qi, kj: (0, 0)),
    ]
    grid_spec = pltpu.PrefetchScalarGridSpec(
        num_scalar_prefetch=2, grid=(batch, len(pairs)), in_specs=in_specs,
        out_specs=pl.BlockSpec((tq, d), lambda b, p, qi, kj: (b * nq + qi[p], 0)),
        scratch_shapes=[pltpu.VMEM((h * tq, 1), F32), pltpu.VMEM((h * tq, 1), F32),
                        pltpu.VMEM((h * tq, kvr), F32), pltpu.VMEM((tq, wo.shape[0]), BF)])
    return pl.pallas_call(
        functools.partial(_flash_body, tq=tq, tk=tk, n_heads=h, hc=hc, c2=scale * LOG2E),
        grid_spec=grid_spec, out_shape=jax.ShapeDtypeStruct((batch * seq, d), F32),
        compiler_params=_cparams(("parallel", "arbitrary")), name="mla_flash_prompt",
    )(qi, kj, qlat, qpe, c_bf, kpe_bf, kst, x, wuv, wo)


def _decode_body(pt_ref, q_ref, qp_ref, *rest, pps, page, n_heads, n_new, kvr, rope, c2):
    kv_refs = rest[:pps]
    ks_refs = rest[pps:2 * pps]
    newkv_ref, newks_ref, o_ref, kbuf, kst_sc, m_sc, l_sc, acc_sc = rest[2 * pps:]
    cidx = pl.program_id(1)
    nc = pl.num_programs(1)
    rows = n_heads * n_new

    @pl.when(cidx == 0)
    def _():
        m_sc[...] = jnp.full(m_sc.shape, NEG_BIG, F32)
        l_sc[...] = jnp.zeros(l_sc.shape, F32)
        acc_sc[...] = jnp.zeros(acc_sc.shape, F32)

    q = q_ref[...]
    qp = qp_ref[:, :rope]

    def attend(k_all, kst, keep):
        ct = k_all[:kvr]
        s_n = _dot(q, ct)
        s_p = _dot(qp, k_all[kvr:kvr + rope])
        kst = kst * c2
        parts = []
        for h in range(n_heads):
            sl = slice(h * n_new, (h + 1) * n_new)
            parts.append(s_n[sl] * kst[h:h + 1, :] + s_p[sl] * c2)
        s = jnp.concatenate(parts, axis=0)
        if keep is not None:
            s = jnp.where(keep, s, NEG_BIG)
        m_prev = m_sc[...]
        m_cur = jnp.maximum(m_prev, jnp.max(s, axis=1, keepdims=True))
        alpha = jnp.exp2(m_prev - m_cur)
        p = jnp.exp2(s - m_cur)
        l_sc[...] = alpha * l_sc[...] + jnp.sum(p, axis=1, keepdims=True)
        acc_sc[...] = alpha * acc_sc[...] + _dot_nt(p.astype(BF), ct)
        m_sc[...] = m_cur

    for p in range(pps):
        kbuf[:, p * page:(p + 1) * page] = kv_refs[p][...].astype(BF)
        kst_sc[:, p * page:(p + 1) * page] = ks_refs[p][...]
    attend(kbuf[...], kst_sc[...], None)

    @pl.when(cidx == nc - 1)
    def _():
        r = lax.broadcasted_iota(jnp.int32, (rows, page), 0)
        kj = lax.broadcasted_iota(jnp.int32, (rows, page), 1)
        keep = kj <= (r % n_new)
        attend(newkv_ref[...].astype(BF), newks_ref[...], keep)
        o_ref[...] = acc_sc[...] * (1.0 / l_sc[...])


def _decode_attend(page_table, qlat_s, qpe_s, cache_kv, j, cache_ks, newkv, newkst, *, pps, scale):
    bd, n_pages = page_table.shape
    rows, kvr = qlat_s.shape[1:]
    width, page = cache_kv.shape[2:]
    n_heads = cache_ks.shape[2]
    n_new = rows // n_heads
    rope = width - kvr
    nc = n_pages // pps
    pt_flat = page_table.reshape(-1)

    def page_map_kv(p):
        return lambda b, c, pt: (j, pt[b * n_pages + c * pps + p], 0, 0)

    in_specs = [pl.BlockSpec((None, rows, kvr), lambda b, c, pt: (b, 0, 0)),
                pl.BlockSpec((None, rows, LANES), lambda b, c, pt: (b, 0, 0))]
    in_specs += [pl.BlockSpec((None, None, width, page), page_map_kv(p)) for p in range(pps)]
    in_specs += [pl.BlockSpec((None, None, n_heads, page), page_map_kv(p)) for p in range(pps)]
    in_specs += [pl.BlockSpec((None, width, page), lambda b, c, pt: (b, 0, 0)),
                 pl.BlockSpec((None, n_heads, page), lambda b, c, pt: (b, 0, 0))]
    grid_spec = pltpu.PrefetchScalarGridSpec(
        num_scalar_prefetch=1, grid=(bd, nc), in_specs=in_specs,
        out_specs=pl.BlockSpec((None, rows, kvr), lambda b, c, pt: (b, 0, 0)),
        scratch_shapes=[pltpu.VMEM((width, pps * page), BF), pltpu.VMEM((n_heads, pps * page), F32),
                        pltpu.VMEM((rows, 1), F32), pltpu.VMEM((rows, 1), F32), pltpu.VMEM((rows, kvr), F32)])
    return pl.pallas_call(
        functools.partial(_decode_body, pps=pps, page=page, n_heads=n_heads, n_new=n_new, kvr=kvr,
                          rope=rope, c2=scale * LOG2E),
        grid_spec=grid_spec, out_shape=jax.ShapeDtypeStruct((bd, rows, kvr), F32),
        compiler_params=_cparams(("parallel", "arbitrary")), name="mla_decode",
    )(pt_flat, qlat_s, qpe_s, *([cache_kv] * pps), *([cache_ks] * pps), newkv, newkst)


def _ffn_dense_body(x_ref, g_ref, w1_ref, w3_ref, w2_ref, o_ref, h_sc):
    j = pl.program_id(1)

    @pl.when(j == 0)
    def _():
        x = x_ref[...]
        h_sc[...] = (_rms(x) * g_ref[...]).astype(BF)
        o_ref[...] = x

    h = h_sc[...]
    a = _dot(h, w1_ref[...])
    b = _dot(h, w3_ref[...])
    g = (a * _sigmoid(a)) * b
    o_ref[...] += _dot(g.astype(BF), w2_ref[...])


def _ffn_dense(x, g, w1, w3, w2, *, bm, bf):
    t, d = x.shape
    ff = w1.shape[1]
    return pl.pallas_call(
        _ffn_dense_body, grid=(t // bm, ff // bf),
        in_specs=[pl.BlockSpec((bm, d), lambda i, j: (i, 0)), _full(g.shape),
                  pl.BlockSpec((d, bf), lambda i, j: (0, j)), pl.BlockSpec((d, bf), lambda i, j: (0, j)),
                  pl.BlockSpec((bf, d), lambda i, j: (j, 0))],
        out_specs=pl.BlockSpec((bm, d), lambda i, j: (i, 0)),
        out_shape=jax.ShapeDtypeStruct((t, d), F32),
        scratch_shapes=[pltpu.VMEM((bm, d), BF)],
        compiler_params=_cparams(("parallel", "arbitrary")), name="ffn_dense",
    )(x, g, w1, w3, w2)


def _ffn_group_body(te_ref, nt_ref, x_ref, w1_ref, w3_ref, w2_ref, o_ref, h_sc):
    i = pl.program_id(0)
    j = pl.program_id(1)

    @pl.when(i < nt_ref[0])
    def _():
        @pl.when(j == 0)
        def _():
            h_sc[...] = x_ref[...].astype(BF)
            o_ref[...] = jnp.zeros(o_ref.shape, F32)

        h = h_sc[...]
        a = _dot(h, w1_ref[...])
        b = _dot(h, w3_ref[...])
        g = (a * _sigmoid(a)) * b
        o_ref[...] += _dot(g.astype(BF), w2_ref[...])


def _ffn_grouped(tile_expert, n_tiles_used, xs, w1, w3, w2, *, bm, bf):
    r, d = xs.shape
    ff = w1.shape[2]
    grid_spec = pltpu.PrefetchScalarGridSpec(
        num_scalar_prefetch=2, grid=(r // bm, ff // bf),
        in_specs=[pl.BlockSpec((bm, d), lambda i, j, te, nt: (i, 0)),
                  pl.BlockSpec((None, d, bf), lambda i, j, te, nt: (te[i], 0, j)),
                  pl.BlockSpec((None, d, bf), lambda i, j, te, nt: (te[i], 0, j)),
                  pl.BlockSpec((None, bf, d), lambda i, j, te, nt: (te[i], j, 0))],
        out_specs=pl.BlockSpec((bm, d), lambda i, j, te, nt: (i, 0)),
        scratch_shapes=[pltpu.VMEM((bm, d), BF)])
    return pl.pallas_call(
        _ffn_group_body, grid_spec=grid_spec, out_shape=jax.ShapeDtypeStruct((r, d), F32),
        compiler_params=_cparams(("parallel", "arbitrary")), name="ffn_grouped",
    )(tile_expert, n_tiles_used, xs, w1, w3, w2)


def _router_body(x_ref, g_ref, wr_ref, tri_ref, f_ref, info_ref, cnt_ref, carry_sc, *, n_experts):
    @pl.when(pl.program_id(0) == 0)
    def _():
        carry_sc[...] = jnp.zeros(carry_sc.shape, F32)

    f = _rms(x_ref[...]) * g_ref[...]
    f_ref[...] = f
    lane = lax.broadcasted_iota(jnp.int32, (f.shape[0], LANES), 1)
    f_hi = f.astype(BF)
    f_lo = (f - f_hi.astype(F32)).astype(BF)
    raw = _dot(f_hi, wr_ref[0]) + (_dot(f_lo, wr_ref[0]) + _dot(f_hi, wr_ref[1]))
    logits = jnp.where(lane < n_experts, raw, -jnp.inf)
    v1 = jnp.max(logits, axis=1, keepdims=True)
    i1 = jnp.min(jnp.where(logits == v1, lane, LANES), axis=1, keepdims=True)
    rest = jnp.where(lane == i1, -jnp.inf, logits)
    v2 = jnp.max(rest, axis=1, keepdims=True)
    i2 = jnp.min(jnp.where(rest == v2, lane, LANES), axis=1, keepdims=True)
    e2 = jnp.exp(v2 - v1)
    den = 1.0 + e2
    g1 = 1.0 / den
    g2 = e2 / den
    hot1 = lane == i1
    hot2 = lane == i2
    onehot = jnp.where(hot1 | hot2, 1.0, 0.0)
    rank = _dot(tri_ref[...], onehot.astype(BF)) + carry_sc[...]
    carry = carry_sc[...] + jnp.sum(onehot, axis=0, keepdims=True)
    carry_sc[...] = carry
    cnt_ref[...] = jnp.broadcast_to(carry, cnt_ref.shape)
    r1 = jnp.sum(jnp.where(hot1, rank, 0.0), axis=1, keepdims=True)
    r2 = jnp.sum(jnp.where(hot2, rank, 0.0), axis=1, keepdims=True)
    info = jnp.zeros((f.shape[0], LANES), F32)
    for k, col in enumerate((i1.astype(F32), i2.astype(F32), r1, r2, g1, g2)):
        info = jnp.where(lane == k, col, info)
    info_ref[...] = info


def _router(x, g, wr, *, bm):
    t, d = x.shape
    n_experts = wr.shape[1]
    wr_f = _pad_lanes(wr.astype(F32))
    wr_hi = wr_f.astype(BF)
    wr_p = jnp.stack([wr_hi, (wr_f - wr_hi.astype(F32)).astype(BF)])
    tri =(jnp.arange(bm)[:, None] > jnp.arange(bm)[None, :]).astype(BF)
    return pl.pallas_call(
        functools.partial(_router_body, n_experts=n_experts), grid=(t // bm,),
        in_specs=[pl.BlockSpec((bm, d), lambda i: (i, 0)), _full(g.shape), _full(wr_p.shape), _full(tri.shape)],
        out_specs=(pl.BlockSpec((bm, d), lambda i: (i, 0)), pl.BlockSpec((bm, LANES), lambda i: (i, 0)),
                   pl.BlockSpec((SUBLANES, LANES), lambda i: (0, 0))),
        out_shape=(jax.ShapeDtypeStruct((t, d), F32), jax.ShapeDtypeStruct((t, LANES), F32),
                   jax.ShapeDtypeStruct((SUBLANES, LANES), F32)),
        scratch_shapes=[pltpu.VMEM((1, LANES), F32)],
        compiler_params=_cparams(("arbitrary",)), name="moe_router",
    )(x, g, wr_p, tri)


SC_WORDS = 256
SC_WINDOW = 128
SC_SUBCORES = 32


def _sc_mesh():
    return plsc.VectorSubcoreMesh(core_axis_name="c", subcore_axis_name="s")


def _sc_dispatch(x, idx_a, idx_b, n_out):
    rows, d = x.shape
    n = idx_a.shape[0]
    last = rows // SC_WINDOW - 1

    @pl.kernel(out_type=jax.ShapeDtypeStruct((n_out, d), x.dtype), mesh=_sc_mesh())
    def k(x_hbm, ia_hbm, ib_hbm, o_hbm):
        for cc in range(d // SC_WORDS):
            cols = pl.ds(cc * SC_WORDS, SC_WORDS)

            def body(x_vmem, ia_vmem, ib_vmem, cols=cols):
                pltpu.sync_copy(x_vmem, o_hbm.at[ia_vmem.at[0], cols])
                pltpu.sync_copy(x_vmem, o_hbm.at[ib_vmem.at[0], cols])

            pltpu.emit_pipeline(
                body, grid=(n // SC_WINDOW,),
                in_specs=[pl.BlockSpec((SC_WINDOW, SC_WORDS), lambda i, cc=cc: (jnp.minimum(i, last), cc)),
                          pl.BlockSpec((1, SC_WINDOW), lambda i: (0, i)),
                          pl.BlockSpec((1, SC_WINDOW), lambda i: (0, i))],
                out_specs=[], core_axis_name=('c', 's'), dimension_semantics=(pltpu.PARALLEL,),
            )(x_hbm, ia_hbm, ib_hbm)

    return k(x, idx_a.reshape(1, n), idx_b.reshape(1, n))


def _sc_gather2(table, idx_a, idx_b):
    n = idx_a.shape[0]
    d = table.shape[1]
    out = jax.ShapeDtypeStruct((n, d), table.dtype)

    @pl.kernel(out_type=(out, out), mesh=_sc_mesh())
    def k(t_hbm, ia_hbm, ib_hbm, oa_hbm, ob_hbm):
        for i_hbm, o_hbm in ((ia_hbm, oa_hbm), (ib_hbm, ob_hbm)):
            for cc in range(d // SC_WORDS):
                cols = pl.ds(cc * SC_WORDS, SC_WORDS)

                def body(i_vmem, o_vmem, cols=cols):
                    pltpu.sync_copy(t_hbm.at[i_vmem.at[0], cols], o_vmem)

                pltpu.emit_pipeline(
                    body, grid=(n // SC_WINDOW,),
                    in_specs=[pl.BlockSpec((1, SC_WINDOW), lambda i: (0, i))],
                    out_specs=[pl.BlockSpec((SC_WINDOW, SC_WORDS), lambda i, cc=cc: (i, cc))],
                    core_axis_name=('c', 's'), dimension_semantics=(pltpu.PARALLEL,),
                )(i_hbm, o_hbm)

    return k(table, idx_a.reshape(1, n), idx_b.reshape(1, n))


def _combine_body(x_ref, ya_ref, yb_ref, info_ref, o_ref):
    info = info_ref[...]
    o_ref[...] = x_ref[...] + (info[:, 4:5] * ya_ref[...] + info[:, 5:6] * yb_ref[...])


def _moe_combine(x, ya, yb, info, *, bm):
    t, d = x.shape
    row = lambda i: (i, 0)
    return pl.pallas_call(
        _combine_body, grid=(t // bm,),
        in_specs=[pl.BlockSpec((bm, d), row), pl.BlockSpec((bm, d), row), pl.BlockSpec((bm, d), row),
                  pl.BlockSpec((bm, LANES), row)],
        out_specs=pl.BlockSpec((bm, d), row), out_shape=jax.ShapeDtypeStruct((t, d), F32),
        compiler_params=_cparams(("parallel",)), name="moe_combine",
    )(x, ya, yb, info)


def _gmlp_body(x_ref, g_ref, win_ref, gv_ref, ws_ref, bs_ref, wout_ref, y_ref, *maybe_v, chunk, groups):
    x = x_ref[...]
    bm = x.shape[0]
    n = (_rms(x) * g_ref[...]).astype(BF)
    z = _gelu(_dot(n, win_ref[...]))
    e = z.shape[1] // 2
    dg = e // groups
    u = z[:, :e]
    v = _rms(z[:, e:]) * gv_ref[...]
    if maybe_v:
        maybe_v[0][...] = v
    vb = v.astype(BF)
    bs = bs_ref[...]
    rows = []
    for c in range(bm // chunk):
        cols = []
        for g in range(groups):
            mixed = _dot(ws_ref[g], vb[c * chunk:(c + 1) * chunk, g * dg:(g + 1) * dg]) + bs[:, g:g + 1]
            cols.append(u[c * chunk:(c + 1) * chunk, g * dg:(g + 1) * dg] * mixed)
        rows.append(jnp.concatenate(cols, axis=1))
    y = jnp.concatenate(rows, axis=0) if len(rows) > 1 else rows[0]
    y_ref[...] = x + _dot(y.astype(BF), wout_ref[...])


def _gmlp(x, g, win, gv, ws, bs, wout, *, bm, emit_v):
    t, d = x.shape
    e = wout.shape[0]
    groups, chunk, _ = ws.shape
    out_shape = [jax.ShapeDtypeStruct((t, d), F32)]
    out_specs = [pl.BlockSpec((bm, d), lambda i: (i, 0))]
    if emit_v:
        out_shape.append(jax.ShapeDtypeStruct((t, e), F32))
        out_specs.append(pl.BlockSpec((bm, e), lambda i: (i, 0)))
    res = pl.pallas_call(
        functools.partial(_gmlp_body, chunk=chunk, groups=groups),
        grid=(t // bm,),
        in_specs=[pl.BlockSpec((bm, d), lambda i: (i, 0)), _full(g.shape), _full(win.shape), _full(gv.shape),
                  _full(ws.shape), _full(bs.shape), _full(wout.shape)],
        out_specs=tuple(out_specs), out_shape=tuple(out_shape),
        compiler_params=_cparams(("parallel",)), name="gmlp_v" if emit_v else "gmlp",
    )(x, g, win, gv, ws, bs, wout)
    return res


def _scan8(a8, u8):
    row = lax.broadcasted_iota(jnp.int32, a8.shape, 0)
    A, B = a8, u8
    for d in (1, 2, 4):
        a_sh = pltpu.roll(A, d, 0)
        b_sh = pltpu.roll(B, d, 0)
        m = row >= d
        B = jnp.where(m, A * b_sh + B, B)
        A = jnp.where(m, A * a_sh, A)
    return A, B


def _rglru_body(x_ref, g_ref, h0_ref, buf_ref, wgate_ref, wx_ref, cw_ref, cb_ref, wa_ref, ba_ref,
                wi_ref, bi_ref, sp_ref, wout_ref, y_ref, hl_ref, xl_ref,
                xe_sc, a_sc, u_sc, hs_sc, carry_sc, *, segmented, gw):
    bt = x_ref.shape[0]
    ec = wx_ref.shape[1]
    x = x_ref[...]
    n = (_rms(x) * g_ref[...]).astype(BF)
    gate = _gelu(_dot(n, wgate_ref[...]))
    xb = _dot(n, wx_ref[...])
    cw = cw_ref[...]
    cb = cb_ref[...]
    ngr = bt // SUBLANES

    if segmented:
        xe_sc[SUBLANES:, :] = xb
        xl_ref[...] = xb

        def conv_group(gi, _):
            base = pl.multiple_of(gi * SUBLANES, SUBLANES)
            cur = xe_sc[pl.ds(SUBLANES + base, SUBLANES), :]
            hist = buf_ref[gi]
            a_sc[0:SUBLANES, :] = hist
            a_sc[SUBLANES:2 * SUBLANES, :] = cur
            acc = cb + cw[3:4, :] * cur
            for dly in (1, 2, 3):
                acc = acc + cw[3 - dly:4 - dly, :] * a_sc[pl.ds(SUBLANES - dly, SUBLANES), :]
            u_sc[pl.ds(base, SUBLANES), :] = acc
            return 0

        lax.fori_loop(0, ngr, conv_group, 0)
        xc = u_sc[...]
    else:
        t_idx = pl.program_id(1)

        @pl.when(t_idx == 0)
        def _():
            xe_sc[0:SUBLANES, :] = jnp.zeros((SUBLANES, ec), F32)
            carry_sc[...] = jnp.zeros(carry_sc.shape, F32)

        xe_sc[SUBLANES:, :] = xb
        xc = cb + cw[3:4, :] * xb
        for dly in (1, 2, 3):
            xc = xc + cw[3 - dly:4 - dly, :] * xe_sc[pl.ds(SUBLANES - dly, bt), :]
        xl_ref[...] = xb[bt - SUBLANES:, :]
        xe_sc[0:SUBLANES, :] = xb[bt - SUBLANES:, :]

    xcb = xc.astype(BF)
    ra, ri = [], []
    for k in range(ec // gw):
        blk = xcb[:, k * gw:(k + 1) * gw]
        ra.append(_dot(blk, wa_ref[k]))
        ri.append(_dot(blk, wi_ref[k]))
    r = _sigmoid(jnp.concatenate(ra, axis=1) + ba_ref[...])
    ig = _sigmoid(jnp.concatenate(ri, axis=1) + bi_ref[...])
    log_a = (-C_RG) * r * sp_ref[...]
    a = jnp.exp(log_a)
    mult = jnp.sqrt(-jnp.tanh(log_a) * (a * a + 1.0))
    a_sc[0:bt, :] = a
    u_sc[...] = mult * ig * xc

    if segmented:
        def scan_group(gi, _):
            base = pl.multiple_of(gi * SUBLANES, SUBLANES)
            A, B = _scan8(a_sc[pl.ds(base, SUBLANES), :], u_sc[pl.ds(base, SUBLANES), :])
            h8 = A * h0_ref[pl.ds(gi, 1), :] + B
            hs_sc[pl.ds(base, SUBLANES), :] = h8
            hl_ref[pl.ds(gi, 1), :] = h8[SUBLANES - 1:SUBLANES, :]
            return 0

        lax.fori_loop(0, ngr, scan_group, 0)
    else:
        def scan_group(gi, carry):
            base = pl.multiple_of(gi * SUBLANES, SUBLANES)
            A, B = _scan8(a_sc[pl.ds(base, SUBLANES), :], u_sc[pl.ds(base, SUBLANES), :])
            h8 = A * carry + B
            hs_sc[pl.ds(base, SUBLANES), :] = h8
            return h8[SUBLANES - 1:SUBLANES, :]

        last = lax.fori_loop(0, ngr, scan_group, carry_sc[...])
        carry_sc[...] = last
        hl_ref[...] = last

    y_ref[...] = x + _dot((hs_sc[...] * gate).astype(BF), wout_ref[...])


def _rglru(x, g, h0, buf8, w, *, bt, batch, seq, segmented):
    d = x.shape[1]
    ec = w['wx'].shape[1]
    gw = w['wa'].shape[1]
    names = ['wgate', 'wx', 'cw', 'cb', 'wa', 'ba', 'wi', 'bi', 'sp', 'wout']
    wspecs = [_full(w[k].shape) for k in names]
    if segmented:
        t = x.shape[0]
        nseq = t // SUBLANES
        sb = bt // SUBLANES
        grid = (t // bt,)
        in_specs = [pl.BlockSpec((bt, d), lambda i: (i, 0)), _full(g.shape),
                    pl.BlockSpec((sb, ec), lambda i: (i, 0)),
                    pl.BlockSpec((sb, SUBLANES, ec), lambda i: (i, 0, 0))] + wspecs
        out_shape = (jax.ShapeDtypeStruct((t, d), F32), jax.ShapeDtypeStruct((nseq, ec), F32),
                     jax.ShapeDtypeStruct((t, ec), F32))
        out_specs = (pl.BlockSpec((bt, d), lambda i: (i, 0)), pl.BlockSpec((sb, ec), lambda i: (i, 0)),
                     pl.BlockSpec((bt, ec), lambda i: (i, 0)))
        sem = ("parallel",)
    else:
        nt = seq // bt
        grid = (batch, nt)
        in_specs = [pl.BlockSpec((bt, d), lambda b, i: (b * nt + i, 0)), _full(g.shape), _full(h0.shape),
                    _full(buf8.shape)] + wspecs
        out_shape = (jax.ShapeDtypeStruct((batch * seq, d), F32), jax.ShapeDtypeStruct((batch, 1, ec), F32),
                     jax.ShapeDtypeStruct((batch, SUBLANES, ec), F32))
        out_specs = (pl.BlockSpec((bt, d), lambda b, i: (b * nt + i, 0)),
                     pl.BlockSpec((None, 1, ec), lambda b, i: (b, 0, 0)),
                     pl.BlockSpec((None, SUBLANES, ec), lambda b, i: (b, 0, 0)))
        sem = ("parallel", "arbitrary")
    scratch = [pltpu.VMEM((bt + SUBLANES, ec), F32), pltpu.VMEM((max(bt, 2 * SUBLANES), ec), F32),
               pltpu.VMEM((bt, ec), F32), pltpu.VMEM((bt, ec), F32), pltpu.VMEM((1, ec), F32)]
    return pl.pallas_call(
        functools.partial(_rglru_body, segmented=segmented, gw=gw),
        grid=grid, in_specs=in_specs, out_specs=out_specs, out_shape=out_shape, scratch_shapes=scratch,
        compiler_params=_cparams(sem), name="rglru_seg" if segmented else "rglru",
    )(x, g, h0, buf8, *[w[k] for k in names])


def _row(v):
    return v.reshape(1, -1).astype(F32)


def _pad_lanes(v, width=LANES):
    return jnp.pad(v, [(0, 0)] * (v.ndim - 1) + [(0, width - v.shape[-1])])


def _prep_mla(g_mix_l, wq_a, g_qa, wq_b, wkv_a, g_kva, w_uk, w_uv, wo, g_qn, g_qr, g_kn, g_kr):
    qr, h, dq = wq_b.shape
    kvr = w_uk.shape[0]
    nope = w_uk.shape[2]
    rope = dq - nope
    w = {
        'gmix': _row(g_mix_l),
        'wqa': wq_a.astype(BF), 'gqa': _row(g_qa),
        'wqn': wq_b[:, :, :nope].reshape(qr, h * nope).astype(BF),
        'wqr': _pad_lanes(wq_b[:, :, nope:]).reshape(qr, h * LANES).astype(BF),
        'wkvc': wkv_a[:, :kvr].astype(BF),
        'wkvr': _pad_lanes(wkv_a[:, kvr:]).astype(BF),
        'gkva': _row(g_kva), 'gkr': _pad_lanes(_row(g_kr)),
        'wukf': w_uk.reshape(kvr, h * nope).astype(BF),
        'wukt': jnp.transpose(w_uk, (1, 2, 0)).astype(BF),
        'gqn': _row(g_qn), 'gqr': _pad_lanes(_row(g_qr)), 'gkn': _row(g_kn),
    }
    wuv = jnp.transpose(w_uv, (1, 0, 2)).astype(BF)
    wo2 = wo.reshape(-1, wo.shape[-1]).astype(BF)
    return w, wuv, wo2, rope


def _rope_tables(pos, rope):
    half = rope // 2
    inv_freq = ROPE_THETA ** (-jnp.arange(half, dtype=F32) / half)
    ang = pos.astype(F32)[:, None] * inv_freq[None, :]
    cos, sin = jnp.cos(ang), jnp.sin(ang)
    cos_t = _pad_lanes(jnp.concatenate([cos, cos], axis=1))
    sin_t = _pad_lanes(jnp.concatenate([-sin, sin], axis=1))
    return cos_t, sin_t


def _mla_layer(x, j, cos_t, sin_t, prm, cache_kv, cache_ks, page_table, dims):
    batch, seq, bd, ns = dims
    tp = batch * seq
    w, wuv, wo2, rope = prm
    h, kvr, _ = wuv.shape
    scale = float(w['wukt'].shape[1] + rope) ** -0.5
    qlat, qpe, ckv, kpe, ks = _mla_proj(x, cos_t, sin_t, w, bm=512)
    kv_row = jnp.concatenate([ckv, kpe[:, :rope]], axis=1)
    ks8 = ks[:, :h]
    c_bf = ckv[:tp].astype(BF)
    kpe_bf = kpe[:tp].astype(BF)
    kst = jnp.transpose(ks8[:tp])
    y_p = _flash_prompt(qlat, qpe, c_bf, kpe_bf, kst, x, wuv, wo2, batch=batch, seq=seq, tq=FLASH_TQ,
                        tk=FLASH_TK, hc=FLASH_HC, scale=scale)
    page = cache_kv.shape[3]
    qlat_s =jnp.transpose(qlat[:, tp:].reshape(h, bd, ns, kvr), (1, 0, 2, 3)).reshape(bd, h * ns, kvr)
    qpe_s = jnp.transpose(qpe[:, tp:].reshape(h, bd, ns, LANES), (1, 0, 2, 3)).reshape(bd, h * ns, LANES)
    newkv = jnp.pad(jnp.transpose(kv_row[tp:].reshape(bd, ns, kvr + rope), (0, 2, 1)),
                    ((0, 0), (0, 0), (0, page - ns)))
    newkst = jnp.pad(jnp.transpose(ks8[tp:].reshape(bd, ns, h), (0, 2, 1)), ((0, 0), (0, 0), (0, page - ns)))
    o_s = _decode_attend(page_table, qlat_s, qpe_s, cache_kv, j, cache_ks, newkv, newkst, pps=DECODE_PPS,
                         scale=scale)
    o_s = jnp.transpose(o_s.reshape(bd, h, ns, kvr), (0, 2, 1, 3)).reshape(bd * ns, h * kvr)
    y_s = _mla_out(o_s, x[tp:], wuv, wo2, bm=min(512, bd * ns))
    y = jnp.concatenate([y_p, y_s], axis=0)
    return y, kv_row, ks8


def _gmlp_layer(x, g_mix_l, w_in, g_v, w_s, b_s, w_out, dims):
    batch, seq, bd, ns = dims
    tp = batch * seq
    groups, chunk, _ = w_s.shape
    tri = jnp.tril(jnp.ones((chunk, chunk), bool))
    ws_p = jnp.where(tri[None], w_s, 0).astype(BF)
    bs_p = _pad_lanes(jnp.transpose(b_s))
    l = min(ns, chunk)
    tri_s = jnp.tril(jnp.ones((l, l), bool))
    ws_small = jnp.where(tri_s[None], w_s[:, :l, :l], 0)
    eye = jnp.eye(chunk // l, dtype=F32)
    ws_s = jnp.einsum('ab,gts->gatbs', eye, ws_small).reshape(groups, chunk, chunk).astype(BF)
    bs_s = _pad_lanes(jnp.tile(jnp.transpose(b_s[:, :l]), (chunk // l, 1)))
    args = (_row(g_mix_l), w_in.astype(BF), _row(g_v))
    wout = w_out.astype(BF)
    (y_p,) = _gmlp(x[:tp], *args, ws_p, bs_p, wout, bm=256, emit_v=False)
    y_s, v_s = _gmlp(x[tp:], *args, ws_s, bs_s, wout, bm=256, emit_v=True)
    return jnp.concatenate([y_p, y_s], axis=0), v_s


def _rglru_layer(x, g_mix_l, h0_s, buf_s, w_gate, w_x, conv_w, conv_b, w_a, b_a, w_i, b_i, lam, w_out, dims):
    batch, seq, bd, ns = dims
    tp = batch * seq
    nb, db, _ = w_a.shape
    ec = nb * db
    pair = 2
    gw = db * pair

    def blockdiag(wb):
        wb = wb.reshape(nb // pair, pair, db, db)
        eye = jnp.eye(pair, dtype=wb.dtype)
        return jnp.einsum('kpde,pq->kpdqe', wb, eye).reshape(nb // pair, gw, gw).astype(BF)

    w = {'wgate': w_gate.astype(BF), 'wx': w_x.astype(BF), 'cw': _pad_rows(conv_w), 'cb': _row(conv_b),
         'wa': blockdiag(w_a), 'ba': _row(b_a), 'wi': blockdiag(w_i), 'bi': _row(b_i),
         'sp': _row(jax.nn.softplus(-lam.astype(F32))), 'wout': w_out.astype(BF)}
    g = _row(g_mix_l)
    cw = conv_w.shape[0]
    zeros_h = jnp.zeros((SUBLANES, ec), F32)
    zeros_b = jnp.zeros((1, SUBLANES, ec), F32)
    y_p, hl_p, xl_p = _rglru(x[:tp], g, zeros_h, zeros_b, w, bt=256, batch=batch, seq=seq, segmented=False)
    buf8 = jnp.pad(buf_s.astype(F32), ((0, 0), (SUBLANES - (cw - 1), 0), (0, 0)))
    y_s, hl_s, xl_s = _rglru(x[tp:], g, h0_s.astype(F32), buf8, w, bt=256, batch=bd, seq=ns, segmented=True)
    h_p = hl_p[:, 0, :]
    conv_p = xl_p[:, SUBLANES - (cw - 1):, :]
    conv_s = xl_s.reshape(bd, ns, ec)[:, ns - (cw - 1):, :]
    return jnp.concatenate([y_p, y_s], axis=0), h_p, conv_p, hl_s, conv_s


def _pad_rows(v, rows=SUBLANES):
    return jnp.pad(v.astype(F32), ((0, rows - v.shape[0]), (0, 0)))


def _moe_layer(x, g_ffn_l, router, w1, w3, w2, *, bm=512, bf=512):
    t, d = x.shape
    ne = router.shape[1]
    f, info, cnt = _router(x, _row(g_ffn_l), router, bm=512)
    counts = cnt[0, :ne].astype(jnp.int32)
    tiles_per = (counts + bm - 1) // bm
    tile_start = jnp.cumsum(tiles_per) - tiles_per
    n_tiles = (t * TOP_K) // bm + ne
    r_pad = n_tiles * bm
    tile_ids = jnp.arange(n_tiles)
    n_used = jnp.sum(tiles_per)
    tile_expert = jnp.sum((tile_ids[:, None] >= tile_start[None, :]).astype(jnp.int32), axis=1) - 1
    last_e = jnp.max(jnp.where(tiles_per > 0, jnp.arange(ne), 0))
    tile_expert = jnp.where(tile_ids < n_used, tile_expert, last_e).astype(jnp.int32)
    base = (tile_start * bm).astype(jnp.int32)
    sel = info[:, :TOP_K].astype(jnp.int32)
    dest = jnp.take(base, sel) + info[:, TOP_K:2 * TOP_K].astype(jnp.int32)
    quantum = SC_WINDOW * SC_SUBCORES
    n_slots = -(-t // quantum) * quantum
    pad = jnp.arange(n_slots - t, dtype=jnp.int32) % SC_WINDOW
    idx_a = jnp.concatenate([dest[:, 0], r_pad + pad])
    idx_b = jnp.concatenate([dest[:, 1], r_pad + pad])
    xs = _sc_dispatch(f, idx_a, idx_b, r_pad + SC_WINDOW)
    ys = _ffn_grouped(tile_expert, n_used.reshape(1).astype(jnp.int32), xs,
                      w1.astype(BF), w3.astype(BF), w2.astype(BF), bm=bm, bf=bf)
    ya, yb = _sc_gather2(ys, jnp.minimum(idx_a, r_pad - 1), jnp.minimum(idx_b, r_pad - 1))
    return _moe_combine(x, ya, yb, info, bm=512)


def kernel(x_prompt, x_sample, cache_mla_kv, cache_mla_kscale, state_rglru_h, state_rglru_conv, page_table, g_mix, g_ffn, mla_wq_a, mla_g_qa, mla_wq_b, mla_wkv_a, mla_g_kva, mla_w_uk, mla_w_uv, mla_wo, mla_g_qn, mla_g_qr, mla_g_kn, mla_g_kr, gm_w_in, gm_g_v, gm_w_s, gm_b_s, gm_w_out, rg_w_gate, rg_w_x, rg_conv_w, rg_conv_b, rg_w_a, rg_b_a, rg_w_i, rg_b_i, rg_lam, rg_w_out, ffd_w1, ffd_w3, ffd_w2, moe_router, moe_w1, moe_w3, moe_w2):
    batch, seq, d = x_prompt.shape
    bd, ns, _ = x_sample.shape
    dims = (batch, seq, bd, ns)
    tp = batch * seq
    depth = g_mix.shape[0]
    n_mixers = 3
    past_len = page_table.shape[1] * cache_mla_kv.shape[2]
    cache_kv_t = jnp.swapaxes(cache_mla_kv, 2, 3)
    cache_ks_t = jnp.swapaxes(cache_mla_kscale, 2, 3)

    x = jnp.concatenate([x_prompt.reshape(tp, d), x_sample.reshape(bd * ns, d)], axis=0)
    rope = mla_wq_b.shape[-1] - mla_w_uk.shape[-1]
    pos = jnp.concatenate([jnp.tile(jnp.arange(seq, dtype=jnp.int32), batch),
                           jnp.tile(past_len + jnp.arange(ns, dtype=jnp.int32), bd)])
    cos_t, sin_t = _rope_tables(pos, rope)

    kv_out, ks_out, v_out, hp_out, cp_out, hs_out, cs_out = [], [], [], [], [], [], []
    counts = [0, 0, 0]
    for layer in range(depth):
        kind = layer % n_mixers
        j = counts[kind]
        counts[kind] += 1
        if kind == 0:
            prm = _prep_mla(g_mix[layer], mla_wq_a[j], mla_g_qa[j], mla_wq_b[j], mla_wkv_a[j], mla_g_kva[j],
                            mla_w_uk[j], mla_w_uv[j], mla_wo[j], mla_g_qn[j], mla_g_qr[j], mla_g_kn[j],
                            mla_g_kr[j])
            x, kv_row, ks8 = _mla_layer(x, j, cos_t, sin_t, prm, cache_kv_t, cache_ks_t, page_table, dims)
            kv_out.append(kv_row)
            ks_out.append(ks8)
        elif kind == 1:
            x, v_s = _gmlp_layer(x, g_mix[layer], gm_w_in[j], gm_g_v[j], gm_w_s[j], gm_b_s[j], gm_w_out[j], dims)
            v_out.append(v_s.reshape(bd, ns, -1))
        else:
            x, h_p, c_p, h_s, c_s = _rglru_layer(
                x, g_mix[layer], state_rglru_h[j], state_rglru_conv[j], rg_w_gate[j], rg_w_x[j], rg_conv_w[j],
                rg_conv_b[j], rg_w_a[j], rg_b_a[j], rg_w_i[j], rg_b_i[j], rg_lam[j], rg_w_out[j], dims)
            hp_out.append(h_p)
            cp_out.append(c_p)
            hs_out.append(h_s)
            cs_out.append(c_s)
        i = layer // 2
        if layer % 2 == 0:
            x = _ffn_dense(x, _row(g_ffn[layer]), ffd_w1[i].astype(BF), ffd_w3[i].astype(BF),
                           ffd_w2[i].astype(BF), bm=512, bf=1408)
        else:
            x = _moe_layer(x, g_ffn[layer], moe_router[i], moe_w1[i], moe_w3[i], moe_w2[i])

    kv_all = jnp.stack(kv_out)
    ks_all = jnp.stack(ks_out)
    return (x[:tp].reshape(batch, seq, d), x[tp:].reshape(bd, ns, d),
            kv_all[:, :tp].reshape(len(kv_out), batch, seq, -1), ks_all[:, :tp].reshape(len(ks_out), batch, seq, -1),
            kv_all[:, tp:].reshape(len(kv_out), bd, ns, -1), ks_all[:, tp:].reshape(len(ks_out), bd, ns, -1),
            jnp.stack(v_out), jnp.stack(hp_out), jnp.stack(cp_out), jnp.stack(hs_out), jnp.stack(cs_out))
```

```python
import functools

import jax
import jax.numpy as jnp
from jax import lax
from jax.experimental import pallas as pl
from jax.experimental.pallas import tpu as pltpu
from jax.experimental.pallas import tpu_sc as plsc

BF = jnp.bfloat16
F32 = jnp.float32
EPS = 1e-6
ROPE_THETA = 10000.0
C_RG = 8.0
TOP_K = 2
NEG_BIG = -1e30
LOG2E = 1.4426950408889634

LANES = 128
SUBLANES = 8
VMEM_LIMIT = 56 * 1024 * 1024

FLASH_TQ = 128
FLASH_TK = 2048
FLASH_HC = 2
DECODE_PPS = 32


def _cparams(sem):
    return pltpu.CompilerParams(dimension_semantics=sem, vmem_limit_bytes=VMEM_LIMIT)


def _dot(a, b):
    return jnp.dot(a, b, preferred_element_type=F32)


def _dot_nt(a, b):
    return lax.dot_general(a, b, (((1,), (1,)), ((), ())), preferred_element_type=F32)


def _rms(x):
    return x * lax.rsqrt(jnp.mean(x * x, axis=-1, keepdims=True) + EPS)


def _rms_half(x):
    return x * lax.rsqrt(jnp.sum(x * x, axis=-1, keepdims=True) * (1.0 / 64.0) + EPS)


def _rope_half(x, cos, sin):
    lane = lax.broadcasted_iota(jnp.int32, x.shape, 1)
    rot = jnp.where(lane < 32, pltpu.roll(x, 96, 1), pltpu.roll(x, 32, 1))
    return x * cos + rot * sin


def _gelu(x):
    return 0.5 * x * (1.0 + jnp.tanh(0.7978845608028654 * (x + 0.044715 * (x * x * x))))


def _sigmoid(x):
    return 1.0 / (1.0 + jnp.exp(-x))


def _full(shape):
    nd = len(shape)
    return pl.BlockSpec(shape, lambda *_: (0,) * nd)


def _mla_proj_body(x_ref, cos_ref, sin_ref, gmix_ref, wqa_ref, gqa_ref, wqn_ref, wqr_ref,
                   wkvc_ref, wkvr_ref, gkva_ref, gkr_ref, wukf_ref, wukt_ref, gqn_ref, gqr_ref,
                   gkn_ref, qlat_ref, qpe_ref, ckv_ref, kpe_ref, ks_ref, *, n_heads):
    x = x_ref[...]
    n = (_rms(x) * gmix_ref[...]).astype(BF)
    cos = cos_ref[...]
    sin = sin_ref[...]
    cq = (_rms(_dot(n, wqa_ref[...])) * gqa_ref[...]).astype(BF)
    ckv = _rms(_dot(n, wkvc_ref[...])) * gkva_ref[...]
    ckv_ref[...] = ckv
    kr = _dot(n, wkvr_ref[...])
    kpe_ref[...] = _rope_half(_rms_half(kr) * gkr_ref[...], cos, sin)
    kn = _dot(ckv.astype(BF), wukf_ref[...])
    lane = lax.broadcasted_iota(jnp.int32, (x.shape[0], LANES), 1)
    ks = jnp.zeros((x.shape[0], LANES), F32)
    for h in range(n_heads):
        blk = kn[:, h * 128:(h + 1) * 128]
        ksh = lax.rsqrt(jnp.mean(blk * blk, axis=-1, keepdims=True) + EPS)
        ks = jnp.where(lane == h, ksh, ks)
    ks_ref[...] = ks
    qn = _dot(cq, wqn_ref[...])
    qr = _dot(cq, wqr_ref[...])
    gq = gqn_ref[...]
    gk = gkn_ref[...]
    for h in range(n_heads):
        a = (_rms(qn[:, h * 128:(h + 1) * 128]) * gq) * gk
        qlat_ref[h] = _dot(a.astype(BF), wukt_ref[h]).astype(BF)
        r = _rms_half(qr[:, h * 128:(h + 1) * 128]) * gqr_ref[...]
        qpe_ref[h] = _rope_half(r, cos, sin).astype(BF)


def _mla_proj(x, cos, sin, w, *, bm):
    t, d = x.shape
    h = w['wukt'].shape[0]
    kvr = w['wkvc'].shape[1]
    row = lambda i: (i, 0)
    in_specs = [pl.BlockSpec((bm, d), row), pl.BlockSpec((bm, LANES), row), pl.BlockSpec((bm, LANES), row)]
    names = ['gmix', 'wqa', 'gqa', 'wqn', 'wqr', 'wkvc', 'wkvr', 'gkva', 'gkr', 'wukf', 'wukt', 'gqn', 'gqr', 'gkn']
    in_specs += [_full(w[k].shape) for k in names]
    out_shape = (jax.ShapeDtypeStruct((h, t, kvr), BF), jax.ShapeDtypeStruct((h, t, LANES), BF),
                 jax.ShapeDtypeStruct((t, kvr), F32), jax.ShapeDtypeStruct((t, LANES), F32),
                 jax.ShapeDtypeStruct((t, LANES), F32))
    out_specs = (pl.BlockSpec((h, bm, kvr), lambda i: (0, i, 0)), pl.BlockSpec((h, bm, LANES), lambda i: (0, i, 0)),
                 pl.BlockSpec((bm, kvr), row), pl.BlockSpec((bm, LANES), row), pl.BlockSpec((bm, LANES), row))
    return pl.pallas_call(
        functools.partial(_mla_proj_body, n_heads=h),
        grid=(t // bm,), in_specs=in_specs, out_specs=out_specs, out_shape=out_shape,
        compiler_params=_cparams(("parallel",)), name="mla_proj",
    )(x, cos, sin, *[w[k] for k in names])


def _mla_out_rows(o_heads, wuv_ref, wo_ref, obuf_ref):
    for h, oh in enumerate(o_heads):
        v = _dot(oh.astype(BF), wuv_ref[h])
        obuf_ref[:, h * 128:(h + 1) * 128] = v.astype(BF)
    return _dot(obuf_ref[...], wo_ref[...])


def _mla_out_body(o_ref, x_ref, wuv_ref, wo_ref, y_ref, obuf_ref, *, n_heads, kvr):
    o = o_ref[...]
    heads = [o[:, h * kvr:(h + 1) * kvr] for h in range(n_heads)]
    y_ref[...] = x_ref[...] + _mla_out_rows(heads, wuv_ref, wo_ref, obuf_ref)


def _mla_out(o, x, wuv, wo, *, bm):
    t, d = x.shape
    h, kvr, vd = wuv.shape
    return pl.pallas_call(
        functools.partial(_mla_out_body, n_heads=h, kvr=kvr),
        grid=(t // bm,),
        in_specs=[pl.BlockSpec((bm, h * kvr), lambda i: (i, 0)), pl.BlockSpec((bm, d), lambda i: (i, 0)),
                  _full(wuv.shape), _full(wo.shape)],
        out_specs=pl.BlockSpec((bm, d), lambda i: (i, 0)),
        out_shape=jax.ShapeDtypeStruct((t, d), F32),
        scratch_shapes=[pltpu.VMEM((bm, h * vd), BF)],
        compiler_params=_cparams(("parallel",)), name="mla_out",
    )(o, x, wuv, wo)


def _flash_body(qi_ref, kj_ref, qlat_ref, qpe_ref, c_ref, kpe_ref, kst_ref, x_ref, wuv_ref, wo_ref, y_ref,
                m_sc, l_sc, acc_sc, obuf_ref, *, tq, tk, n_heads, hc, c2):
    pidx = pl.program_id(1)
    i = qi_ref[pidx]
    j = kj_ref[pidx]
    last_needed = (i * tq + tq - 1) // tk
    kvr = qlat_ref.shape[-1]

    @pl.when(j == 0)
    def _():
        m_sc[...] = jnp.full(m_sc.shape, NEG_BIG, F32)
        l_sc[...] = jnp.zeros(l_sc.shape, F32)
        acc_sc[...] = jnp.zeros(acc_sc.shape, F32)

    def step(masked):
        c = c_ref[...]
        kpe = kpe_ref[...]
        kst = kst_ref[...] * c2
        if masked:
            qpos = i * tq + lax.broadcasted_iota(jnp.int32, (tq, tk), 0)
            kpos = j * tk + lax.broadcasted_iota(jnp.int32, (tq, tk), 1)
            keep = kpos <= qpos
        for g in range(n_heads // hc):
            r0, nr = g * hc * tq, hc * tq
            q = qlat_ref[g * hc:(g + 1) * hc].reshape(nr, kvr)
            qp = qpe_ref[g * hc:(g + 1) * hc].reshape(nr, LANES)
            s_n = _dot_nt(q, c)
            s_p = _dot_nt(qp, kpe)
            p_rows = []
            for hh in range(hc):
                h = g * hc + hh
                hr = slice(hh * tq, (hh + 1) * tq)
                pieces = []
                mx = None
                for kt in range(tk // LANES):
                    cols = slice(kt * LANES, (kt + 1) * LANES)
                    sh = s_n[hr, cols] * kst[h:h + 1, cols] + s_p[hr, cols] * c2
                    if masked:
                        sh = jnp.where(keep[:, cols], sh, NEG_BIG)
                    pieces.append(sh)
                    mx = sh if mx is None else jnp.maximum(mx, sh)
                hs = slice(h * tq, (h + 1) * tq)
                m_prev = m_sc[hs]
                m_cur = jnp.maximum(m_prev, jnp.max(mx, axis=1, keepdims=True))
                alpha = jnp.exp2(m_prev - m_cur)
                ps = []
                sm = None
                for sh in pieces:
                    p = jnp.exp2(sh - m_cur)
                    sm = p if sm is None else sm + p
                    ps.append(p.astype(BF))
                l_sc[hs] = alpha * l_sc[hs] + jnp.sum(sm, axis=1, keepdims=True)
                m_sc[hs] = m_cur
                acc_sc[hs] = alpha * acc_sc[hs]
                p_rows.append(jnp.concatenate(ps, axis=1))
            p_all = jnp.concatenate(p_rows, axis=0) if hc > 1 else p_rows[0]
            acc_sc[r0:r0 + nr] += _dot(p_all, c)

    @pl.when(j < last_needed)
    def _():
        step(False)

    @pl.when(j == last_needed)
    def _():
        step(True)
        inv = 1.0 / l_sc[...]
        heads = [acc_sc[h * tq:(h + 1) * tq, :] * inv[h * tq:(h + 1) * tq] for h in range(n_heads)]
        y_ref[...] = x_ref[...] + _mla_out_rows(heads, wuv_ref, wo_ref, obuf_ref)


def _flash_prompt(qlat, qpe, c_bf, kpe_bf, kst, x, wuv, wo, *, batch, seq, tq, tk, hc, scale):
    h, _, kvr = qlat.shape
    d = x.shape[1]
    nq = seq // tq
    nk = seq // tk
    pairs = [(i, j) for i in range(nq) for j in range((i * tq + tq - 1) // tk + 1)]
    qi = jnp.asarray([p[0] for p in pairs], jnp.int32)
    kj = jnp.asarray([p[1] for p in pairs], jnp.int32)
    in_specs = [
        pl.BlockSpec((h, tq, kvr), lambda b, p, qi, kj: (0, b * nq + qi[p], 0)),
        pl.BlockSpec((h, tq, LANES), lambda b, p, qi, kj: (0, b * nq + qi[p], 0)),
        pl.BlockSpec((tk, kvr), lambda b, p, qi, kj: (b * nk + kj[p], 0)),
        pl.BlockSpec((tk, LANES), lambda b, p, qi, kj: (b * nk + kj[p], 0)),
        pl.BlockSpec((h, tk), lambda b, p, qi, kj: (0, b * nk + kj[p])),
        pl.BlockSpec((tq, d), lambda b, p, qi, kj: (b * nq + qi[p], 0)),
        pl.BlockSpec(wuv.shape, lambda b, p, qi, kj: (0, 0, 0)),
        pl.BlockSpec(wo.shape, lambda b, p, qi, kj: (0, 0)),
    ]
    grid_spec = pltpu.PrefetchScalarGridSpec(
        num_scalar_prefetch=2, grid=(batch, len(pairs)), in_specs=in_specs,
        out_specs=pl.BlockSpec((tq, d), lambda b, p, qi, kj: (b * nq + qi[p], 0)),
        scratch_shapes=[pltpu.VMEM((h * tq, 1), F32), pltpu.VMEM((h * tq, 1), F32),
                        pltpu.VMEM((h * tq, kvr), F32), pltpu.VMEM((tq, wo.shape[0]), BF)])
    return pl.pallas_call(
        functools.partial(_flash_body, tq=tq, tk=tk, n_heads=h, hc=hc, c2=scale * LOG2E),
        grid_spec=grid_spec, out_shape=jax.ShapeDtypeStruct((batch * seq, d), F32),
        compiler_params=_cparams(("parallel", "arbitrary")), name="mla_flash_prompt",
    )(qi, kj, qlat, qpe, c_bf, kpe_bf, kst, x, wuv, wo)


def _decode_body(pt_ref, q_ref, qp_ref, *rest, pps, page, n_heads, n_new, kvr, rope, c2):
    kv_refs = rest[:pps]
    ks_refs = rest[pps:2 * pps]
    newkv_ref, newks_ref, o_ref, kbuf, kst_sc, m_sc, l_sc, acc_sc = rest[2 * pps:]
    cidx = pl.program_id(1)
    nc = pl.num_programs(1)
    rows = n_heads * n_new

    @pl.when(cidx == 0)
    def _():
        m_sc[...] = jnp.full(m_sc.shape, NEG_BIG, F32)
        l_sc[...] = jnp.zeros(l_sc.shape, F32)
        acc_sc[...] = jnp.zeros(acc_sc.shape, F32)

    q = q_ref[...]
    qp = qp_ref[:, :rope]

    def attend(k_all, kst, keep):
        ct = k_all[:kvr]
        s_n = _dot(q, ct)
        s_p = _dot(qp, k_all[kvr:kvr + rope])
        kst = kst * c2
        parts = []
        for h in range(n_heads):
            sl = slice(h * n_new, (h + 1) * n_new)
            parts.append(s_n[sl] * kst[h:h + 1, :] + s_p[sl] * c2)
        s = jnp.concatenate(parts, axis=0)
        if keep is not None:
            s = jnp.where(keep, s, NEG_BIG)
        m_prev = m_sc[...]
        m_cur = jnp.maximum(m_prev, jnp.max(s, axis=1, keepdims=True))
        alpha = jnp.exp2(m_prev - m_cur)
        p = jnp.exp2(s - m_cur)
        l_sc[...] = alpha * l_sc[...] + jnp.sum(p, axis=1, keepdims=True)
        acc_sc[...] = alpha * acc_sc[...] + _dot_nt(p.astype(BF), ct)
        m_sc[...] = m_cur

    for p in range(pps):
        kbuf[:, p * page:(p + 1) * page] = kv_refs[p][...].astype(BF)
        kst_sc[:, p * page:(p + 1) * page] = ks_refs[p][...]
    attend(kbuf[...], kst_sc[...], None)

    @pl.when(cidx == nc - 1)
    def _():
        r = lax.broadcasted_iota(jnp.int32, (rows, page), 0)
        kj = lax.broadcasted_iota(jnp.int32, (rows, page), 1)
        keep = kj <= (r % n_new)
        attend(newkv_ref[...].astype(BF), newks_ref[...], keep)
        o_ref[...] = acc_sc[...] * (1.0 / l_sc[...])


def _decode_attend(page_table, qlat_s, qpe_s, cache_kv, j, cache_ks, newkv, newkst, *, pps, scale):
    bd, n_pages = page_table.shape
    rows, kvr = qlat_s.shape[1:]
    width, page = cache_kv.shape[2:]
    n_heads = cache_ks.shape[2]
    n_new = rows // n_heads
    rope = width - kvr
    nc = n_pages // pps
    pt_flat = page_table.reshape(-1)

    def page_map_kv(p):
        return lambda b, c, pt: (j, pt[b * n_pages + c * pps + p], 0, 0)

    in_specs = [pl.BlockSpec((None, rows, kvr), lambda b, c, pt: (b, 0, 0)),
                pl.BlockSpec((None, rows, LANES), lambda b, c, pt: (b, 0, 0))]
    in_specs += [pl.BlockSpec((None, None, width, page), page_map_kv(p)) for p in range(pps)]
    in_specs += [pl.BlockSpec((None, None, n_heads, page), page_map_kv(p)) for p in range(pps)]
    in_specs += [pl.BlockSpec((None, width, page), lambda b, c, pt: (b, 0, 0)),
                 pl.BlockSpec((None, n_heads, page), lambda b, c, pt: (b, 0, 0))]
    grid_spec = pltpu.PrefetchScalarGridSpec(
        num_scalar_prefetch=1, grid=(bd, nc), in_specs=in_specs,
        out_specs=pl.BlockSpec((None, rows, kvr), lambda b, c, pt: (b, 0, 0)),
        scratch_shapes=[pltpu.VMEM((width, pps * page), BF), pltpu.VMEM((n_heads, pps * page), F32),
                        pltpu.VMEM((rows, 1), F32), pltpu.VMEM((rows, 1), F32), pltpu.VMEM((rows, kvr), F32)])
    return pl.pallas_call(
        functools.partial(_decode_body, pps=pps, page=page, n_heads=n_heads, n_new=n_new, kvr=kvr,
                          rope=rope, c2=scale * LOG2E),
        grid_spec=grid_spec, out_shape=jax.ShapeDtypeStruct((bd, rows, kvr), F32),
        compiler_params=_cparams(("parallel", "arbitrary")), name="mla_decode",
    )(pt_flat, qlat_s, qpe_s, *([cache_kv] * pps), *([cache_ks] * pps), newkv, newkst)


def _ffn_dense_body(x_ref, g_ref, w1_ref, w3_ref, w2_ref, o_ref, h_sc):
    j = pl.program_id(1)

    @pl.when(j == 0)
    def _():
        x = x_ref[...]
        h_sc[...] = (_rms(x) * g_ref[...]).astype(BF)
        o_ref[...] = x

    h = h_sc[...]
    a = _dot(h, w1_ref[...])
    b = _dot(h, w3_ref[...])
    g = (a * _sigmoid(a)) * b
    o_ref[...] += _dot(g.astype(BF), w2_ref[...])


def _ffn_dense(x, g, w1, w3, w2, *, bm, bf):
    t, d = x.shape
    ff = w1.shape[1]
    return pl.pallas_call(
        _ffn_dense_body, grid=(t // bm, ff // bf),
        in_specs=[pl.BlockSpec((bm, d), lambda i, j: (i, 0)), _full(g.shape),
                  pl.BlockSpec((d, bf), lambda i, j: (0, j)), pl.BlockSpec((d, bf), lambda i, j: (0, j)),
                  pl.BlockSpec((bf, d), lambda i, j: (j, 0))],
        out_specs=pl.BlockSpec((bm, d), lambda i, j: (i, 0)),
        out_shape=jax.ShapeDtypeStruct((t, d), F32),
        scratch_shapes=[pltpu.VMEM((bm, d), BF)],
        compiler_params=_cparams(("parallel", "arbitrary")), name="ffn_dense",
    )(x, g, w1, w3, w2)


def _ffn_group_body(te_ref, nt_ref, x_ref, w1_ref, w3_ref, w2_ref, o_ref, h_sc):
    i = pl.program_id(0)
    j = pl.program_id(1)

    @pl.when(i < nt_ref[0])
    def _():
        npc, _, w = x_ref.shape

        @pl.when(j == 0)
        def _():
            for c in range(npc):
                h_sc[:, c * w:(c + 1) * w] = x_ref[c].astype(BF)
            o_ref[...] = jnp.zeros(o_ref.shape, F32)

        h = h_sc[...]
        a = _dot(h, w1_ref[...])
        b = _dot(h, w3_ref[...])
        g = (a * _sigmoid(a)) * b
        y = _dot(g.astype(BF), w2_ref[...])
        for c in range(npc):
            o_ref[c] += y[:, c * w:(c + 1) * w]


def _ffn_grouped(tile_expert, n_tiles_used, xs, w1, w3, w2, *, bm, bf):
    npc, r, w = xs.shape
    d = npc * w
    ff = w1.shape[2]
    grid_spec = pltpu.PrefetchScalarGridSpec(
        num_scalar_prefetch=2, grid=(r // bm, ff // bf),
        in_specs=[pl.BlockSpec((npc, bm, w), lambda i, j, te, nt: (0, i, 0)),
                  pl.BlockSpec((None, d, bf), lambda i, j, te, nt: (te[i], 0, j)),
                  pl.BlockSpec((None, d, bf), lambda i, j, te, nt: (te[i], 0, j)),
                  pl.BlockSpec((None, bf, d), lambda i, j, te, nt: (te[i], j, 0))],
        out_specs=pl.BlockSpec((npc, bm, w), lambda i, j, te, nt: (0, i, 0)),
        scratch_shapes=[pltpu.VMEM((bm, d), BF)])
    return pl.pallas_call(
        _ffn_group_body, grid_spec=grid_spec, out_shape=jax.ShapeDtypeStruct((npc, r, w), F32),
        compiler_params=_cparams(("parallel", "arbitrary")), name="ffn_grouped",
    )(tile_expert, n_tiles_used, xs, w1, w3, w2)


def _router_body(x_ref, g_ref, wr_ref, tri_ref, f_ref, info_ref, cnt_ref, carry_sc, *, n_experts):
    @pl.when(pl.program_id(0) == 0)
    def _():
        carry_sc[...] = jnp.zeros(carry_sc.shape, F32)

    f = _rms(x_ref[...]) * g_ref[...]
    for c in range(f_ref.shape[0]):
        f_ref[c] = f[:, c * SC_WORDS:(c + 1) * SC_WORDS]
    lane = lax.broadcasted_iota(jnp.int32, (f.shape[0], LANES), 1)
    f_hi = f.astype(BF)
    f_lo = (f - f_hi.astype(F32)).astype(BF)
    raw = _dot(f_hi, wr_ref[0]) + (_dot(f_lo, wr_ref[0]) + _dot(f_hi, wr_ref[1]))
    logits = jnp.where(lane < n_experts, raw, -jnp.inf)
    v1 = jnp.max(logits, axis=1, keepdims=True)
    i1 = jnp.min(jnp.where(logits == v1, lane, LANES), axis=1, keepdims=True)
    rest = jnp.where(lane == i1, -jnp.inf, logits)
    v2 = jnp.max(rest, axis=1, keepdims=True)
    i2 = jnp.min(jnp.where(rest == v2, lane, LANES), axis=1, keepdims=True)
    e2 = jnp.exp(v2 - v1)
    den = 1.0 + e2
    g1 = 1.0 / den
    g2 = e2 / den
    hot1 = lane == i1
    hot2 = lane == i2
    onehot = jnp.where(hot1 | hot2, 1.0, 0.0)
    rank = _dot(tri_ref[...], onehot.astype(BF)) + carry_sc[...]
    carry = carry_sc[...] + jnp.sum(onehot, axis=0, keepdims=True)
    carry_sc[...] = carry
    cnt_ref[...] = jnp.broadcast_to(carry, cnt_ref.shape)
    r1 = jnp.sum(jnp.where(hot1, rank, 0.0), axis=1, keepdims=True)
    r2 = jnp.sum(jnp.where(hot2, rank, 0.0), axis=1, keepdims=True)
    info = jnp.zeros((f.shape[0], LANES), F32)
    for k, col in enumerate((i1.astype(F32), i2.astype(F32), r1, r2, g1, g2)):
        info = jnp.where(lane == k, col, info)
    info_ref[...] = info


def _router(x, g, wr, *, bm):
    t, d = x.shape
    n_experts = wr.shape[1]
    wr_f = _pad_lanes(wr.astype(F32))
    wr_hi = wr_f.astype(BF)
    wr_p = jnp.stack([wr_hi, (wr_f - wr_hi.astype(F32)).astype(BF)])
    tri =(jnp.arange(bm)[:, None] > jnp.arange(bm)[None, :]).astype(BF)
    return pl.pallas_call(
        functools.partial(_router_body, n_experts=n_experts), grid=(t // bm,),
        in_specs=[pl.BlockSpec((bm, d), lambda i: (i, 0)), _full(g.shape), _full(wr_p.shape), _full(tri.shape)],
        out_specs=(pl.BlockSpec((d // SC_WORDS, bm, SC_WORDS), lambda i: (0, i, 0)),
                   pl.BlockSpec((bm, LANES), lambda i: (i, 0)),
                   pl.BlockSpec((SUBLANES, LANES), lambda i: (0, 0))),
        out_shape=(jax.ShapeDtypeStruct((d // SC_WORDS, t, SC_WORDS), F32), jax.ShapeDtypeStruct((t, LANES), F32),
                   jax.ShapeDtypeStruct((SUBLANES, LANES), F32)),
        scratch_shapes=[pltpu.VMEM((1, LANES), F32)],
        compiler_params=_cparams(("arbitrary",)), name="moe_router",
    )(x, g, wr_p, tri)


SC_WORDS = 256
SC_WINDOW = 128


def _sc_mesh():
    return plsc.VectorSubcoreMesh(core_axis_name="c", subcore_axis_name="s")


def _sc_dispatch(pieces, idx_a, idx_b, n_out):
    n, w = pieces.shape

    @pl.kernel(out_type=jax.ShapeDtypeStruct((n_out, w), pieces.dtype), mesh=_sc_mesh())
    def k(x_hbm, ia_hbm, ib_hbm, o_hbm):
        def body(x_vmem, ia_vmem, ib_vmem):
            pltpu.sync_copy(x_vmem, o_hbm.at[ia_vmem.at[0]])
            pltpu.sync_copy(x_vmem, o_hbm.at[ib_vmem.at[0]])

        pltpu.emit_pipeline(
            body, grid=(n // SC_WINDOW,),
            in_specs=[pl.BlockSpec((SC_WINDOW, w), lambda i: (i, 0)),
                      pl.BlockSpec((1, SC_WINDOW), lambda i: (0, i)),
                      pl.BlockSpec((1, SC_WINDOW), lambda i: (0, i))],
            out_specs=[], core_axis_name=('c', 's'), dimension_semantics=(pltpu.PARALLEL,),
        )(x_hbm, ia_hbm, ib_hbm)

    return k(pieces, idx_a.reshape(1, n), idx_b.reshape(1, n))


def _sc_gather(table, idx):
    n = idx.shape[0]
    w = table.shape[1]

    @pl.kernel(out_type=jax.ShapeDtypeStruct((n, w), table.dtype), mesh=_sc_mesh())
    def k(t_hbm, i_hbm, o_hbm):
        def body(i_vmem, o_vmem):
            pltpu.sync_copy(t_hbm.at[i_vmem.at[0]], o_vmem)

        pltpu.emit_pipeline(
            body, grid=(n // SC_WINDOW,),
            in_specs=[pl.BlockSpec((1, SC_WINDOW), lambda i: (0, i))],
            out_specs=[pl.BlockSpec((SC_WINDOW, w), lambda i: (i, 0))],
            core_axis_name=('c', 's'), dimension_semantics=(pltpu.PARALLEL,),
        )(i_hbm, o_hbm)

    return k(table, idx.reshape(1, n))


def _combine_body(x_ref, ya_ref, yb_ref, info_ref, o_ref):
    info = info_ref[...]
    ga, gb = info[:, 4:5], info[:, 5:6]
    npc, _, w = ya_ref.shape
    for c in range(npc):
        cols = slice(c * w, (c + 1) * w)
        o_ref[:, cols] = x_ref[:, cols] + (ga * ya_ref[c] + gb * yb_ref[c])


def _moe_combine(x, ya, yb, info, *, bm):
    t, d = x.shape
    npc, _, w = ya.shape
    row = lambda i: (i, 0)
    blk = pl.BlockSpec((npc, bm, w), lambda i: (0, i, 0))
    return pl.pallas_call(
        _combine_body, grid=(t // bm,),
        in_specs=[pl.BlockSpec((bm, d), row), blk, blk, pl.BlockSpec((bm, LANES), row)],
        out_specs=pl.BlockSpec((bm, d), row), out_shape=jax.ShapeDtypeStruct((t, d), F32),
        compiler_params=_cparams(("parallel",)), name="moe_combine",
    )(x, ya, yb, info)


def _gmlp_body(x_ref, g_ref, win_ref, gv_ref, ws_ref, bs_ref, wout_ref, y_ref, *maybe_v, chunk, groups):
    x = x_ref[...]
    bm = x.shape[0]
    n = (_rms(x) * g_ref[...]).astype(BF)
    z = _gelu(_dot(n, win_ref[...]))
    e = z.shape[1] // 2
    dg = e // groups
    u = z[:, :e]
    v = _rms(z[:, e:]) * gv_ref[...]
    if maybe_v:
        maybe_v[0][...] = v
    vb = v.astype(BF)
    bs = bs_ref[...]
    rows = []
    for c in range(bm // chunk):
        cols = []
        for g in range(groups):
            mixed = _dot(ws_ref[g], vb[c * chunk:(c + 1) * chunk, g * dg:(g + 1) * dg]) + bs[:, g:g + 1]
            cols.append(u[c * chunk:(c + 1) * chunk, g * dg:(g + 1) * dg] * mixed)
        rows.append(jnp.concatenate(cols, axis=1))
    y = jnp.concatenate(rows, axis=0) if len(rows) > 1 else rows[0]
    y_ref[...] = x + _dot(y.astype(BF), wout_ref[...])


def _gmlp(x, g, win, gv, ws, bs, wout, *, bm, emit_v):
    t, d = x.shape
    e = wout.shape[0]
    groups, chunk, _ = ws.shape
    out_shape = [jax.ShapeDtypeStruct((t, d), F32)]
    out_specs = [pl.BlockSpec((bm, d), lambda i: (i, 0))]
    if emit_v:
        out_shape.append(jax.ShapeDtypeStruct((t, e), F32))
        out_specs.append(pl.BlockSpec((bm, e), lambda i: (i, 0)))
    res = pl.pallas_call(
        functools.partial(_gmlp_body, chunk=chunk, groups=groups),
        grid=(t // bm,),
        in_specs=[pl.BlockSpec((bm, d), lambda i: (i, 0)), _full(g.shape), _full(win.shape), _full(gv.shape),
                  _full(ws.shape), _full(bs.shape), _full(wout.shape)],
        out_specs=tuple(out_specs), out_shape=tuple(out_shape),
        compiler_params=_cparams(("parallel",)), name="gmlp_v" if emit_v else "gmlp",
    )(x, g, win, gv, ws, bs, wout)
    return res


def _scan8(a8, u8):
    row = lax.broadcasted_iota(jnp.int32, a8.shape, 0)
    A, B = a8, u8
    for d in (1, 2, 4):
        a_sh = pltpu.roll(A, d, 0)
        b_sh = pltpu.roll(B, d, 0)
        m = row >= d
        B = jnp.where(m, A * b_sh + B, B)
        A = jnp.where(m, A * a_sh, A)
    return A, B


def _rglru_body(x_ref, g_ref, h0_ref, buf_ref, wgate_ref, wx_ref, cw_ref, cb_ref, wa_ref, ba_ref,
                wi_ref, bi_ref, sp_ref, wout_ref, y_ref, hl_ref, xl_ref,
                xe_sc, a_sc, u_sc, hs_sc, carry_sc, *, segmented, gw):
    bt = x_ref.shape[0]
    ec = wx_ref.shape[1]
    x = x_ref[...]
    n = (_rms(x) * g_ref[...]).astype(BF)
    gate = _gelu(_dot(n, wgate_ref[...]))
    xb = _dot(n, wx_ref[...])
    cw = cw_ref[...]
    cb = cb_ref[...]
    ngr = bt // SUBLANES

    if segmented:
        xe_sc[SUBLANES:, :] = xb
        xl_ref[...] = xb

        def conv_group(gi, _):
            base = pl.multiple_of(gi * SUBLANES, SUBLANES)
            cur = xe_sc[pl.ds(SUBLANES + base, SUBLANES), :]
            hist = buf_ref[gi]
            a_sc[0:SUBLANES, :] = hist
            a_sc[SUBLANES:2 * SUBLANES, :] = cur
            acc = cb + cw[3:4, :] * cur
            for dly in (1, 2, 3):
                acc = acc + cw[3 - dly:4 - dly, :] * a_sc[pl.ds(SUBLANES - dly, SUBLANES), :]
            u_sc[pl.ds(base, SUBLANES), :] = acc
            return 0

        lax.fori_loop(0, ngr, conv_group, 0)
        xc = u_sc[...]
    else:
        t_idx = pl.program_id(1)

        @pl.when(t_idx == 0)
        def _():
            xe_sc[0:SUBLANES, :] = jnp.zeros((SUBLANES, ec), F32)
            carry_sc[...] = jnp.zeros(carry_sc.shape, F32)

        xe_sc[SUBLANES:, :] = xb
        xc = cb + cw[3:4, :] * xb
        for dly in (1, 2, 3):
            xc = xc + cw[3 - dly:4 - dly, :] * xe_sc[pl.ds(SUBLANES - dly, bt), :]
        xl_ref[...] = xb[bt - SUBLANES:, :]
        xe_sc[0:SUBLANES, :] = xb[bt - SUBLANES:, :]

    xcb = xc.astype(BF)
    ra, ri = [], []
    for k in range(ec // gw):
        blk = xcb[:, k * gw:(k + 1) * gw]
        ra.append(_dot(blk, wa_ref[k]))
        ri.append(_dot(blk, wi_ref[k]))
    r = _sigmoid(jnp.concatenate(ra, axis=1) + ba_ref[...])
    ig = _sigmoid(jnp.concatenate(ri, axis=1) + bi_ref[...])
    log_a = (-C_RG) * r * sp_ref[...]
    a = jnp.exp(log_a)
    mult = jnp.sqrt(-jnp.tanh(log_a) * (a * a + 1.0))
    a_sc[0:bt, :] = a
    u_sc[...] = mult * ig * xc

    if segmented:
        def scan_group(gi, _):
            base = pl.multiple_of(gi * SUBLANES, SUBLANES)
            A, B = _scan8(a_sc[pl.ds(base, SUBLANES), :], u_sc[pl.ds(base, SUBLANES), :])
            h8 = A * h0_ref[pl.ds(gi, 1), :] + B
            hs_sc[pl.ds(base, SUBLANES), :] = h8
            hl_ref[pl.ds(gi, 1), :] = h8[SUBLANES - 1:SUBLANES, :]
            return 0

        lax.fori_loop(0, ngr, scan_group, 0)
    else:
        def scan_group(gi, carry):
            base = pl.multiple_of(gi * SUBLANES, SUBLANES)
            A, B = _scan8(a_sc[pl.ds(base, SUBLANES), :], u_sc[pl.ds(base, SUBLANES), :])
            h8 = A * carry + B
            hs_sc[pl.ds(base, SUBLANES), :] = h8
            return h8[SUBLANES - 1:SUBLANES, :]

        last = lax.fori_loop(0, ngr, scan_group, carry_sc[...])
        carry_sc[...] = last
        hl_ref[...] = last

    y_ref[...] = x + _dot((hs_sc[...] * gate).astype(BF), wout_ref[...])


def _rglru(x, g, h0, buf8, w, *, bt, batch, seq, segmented):
    d = x.shape[1]
    ec = w['wx'].shape[1]
    gw = w['wa'].shape[1]
    names = ['wgate', 'wx', 'cw', 'cb', 'wa', 'ba', 'wi', 'bi', 'sp', 'wout']
    wspecs = [_full(w[k].shape) for k in names]
    if segmented:
        t = x.shape[0]
        nseq = t // SUBLANES
        sb = bt // SUBLANES
        grid = (t // bt,)
        in_specs = [pl.BlockSpec((bt, d), lambda i: (i, 0)), _full(g.shape),
                    pl.BlockSpec((sb, ec), lambda i: (i, 0)),
                    pl.BlockSpec((sb, SUBLANES, ec), lambda i: (i, 0, 0))] + wspecs
        out_shape = (jax.ShapeDtypeStruct((t, d), F32), jax.ShapeDtypeStruct((nseq, ec), F32),
                     jax.ShapeDtypeStruct((t, ec), F32))
        out_specs = (pl.BlockSpec((bt, d), lambda i: (i, 0)), pl.BlockSpec((sb, ec), lambda i: (i, 0)),
                     pl.BlockSpec((bt, ec), lambda i: (i, 0)))
        sem = ("parallel",)
    else:
        nt = seq // bt
        grid = (batch, nt)
        in_specs = [pl.BlockSpec((bt, d), lambda b, i: (b * nt + i, 0)), _full(g.shape), _full(h0.shape),
                    _full(buf8.shape)] + wspecs
        out_shape = (jax.ShapeDtypeStruct((batch * seq, d), F32), jax.ShapeDtypeStruct((batch, 1, ec), F32),
                     jax.ShapeDtypeStruct((batch, SUBLANES, ec), F32))
        out_specs = (pl.BlockSpec((bt, d), lambda b, i: (b * nt + i, 0)),
                     pl.BlockSpec((None, 1, ec), lambda b, i: (b, 0, 0)),
                     pl.BlockSpec((None, SUBLANES, ec), lambda b, i: (b, 0, 0)))
        sem = ("parallel", "arbitrary")
    scratch = [pltpu.VMEM((bt + SUBLANES, ec), F32), pltpu.VMEM((max(bt, 2 * SUBLANES), ec), F32),
               pltpu.VMEM((bt, ec), F32), pltpu.VMEM((bt, ec), F32), pltpu.VMEM((1, ec), F32)]
    return pl.pallas_call(
        functools.partial(_rglru_body, segmented=segmented, gw=gw),
        grid=grid, in_specs=in_specs, out_specs=out_specs, out_shape=out_shape, scratch_shapes=scratch,
        compiler_params=_cparams(sem), name="rglru_seg" if segmented else "rglru",
    )(x, g, h0, buf8, *[w[k] for k in names])


def _row(v):
    return v.reshape(1, -1).astype(F32)


def _pad_lanes(v, width=LANES):
    return jnp.pad(v, [(0, 0)] * (v.ndim - 1) + [(0, width - v.shape[-1])])


def _prep_mla(g_mix_l, wq_a, g_qa, wq_b, wkv_a, g_kva, w_uk, w_uv, wo, g_qn, g_qr, g_kn, g_kr):
    qr, h, dq = wq_b.shape
    kvr = w_uk.shape[0]
    nope = w_uk.shape[2]
    rope = dq - nope
    w = {
        'gmix': _row(g_mix_l),
        'wqa': wq_a.astype(BF), 'gqa': _row(g_qa),
        'wqn': wq_b[:, :, :nope].reshape(qr, h * nope).astype(BF),
        'wqr': _pad_lanes(wq_b[:, :, nope:]).reshape(qr, h * LANES).astype(BF),
        'wkvc': wkv_a[:, :kvr].astype(BF),
        'wkvr': _pad_lanes(wkv_a[:, kvr:]).astype(BF),
        'gkva': _row(g_kva), 'gkr': _pad_lanes(_row(g_kr)),
        'wukf': w_uk.reshape(kvr, h * nope).astype(BF),
        'wukt': jnp.transpose(w_uk, (1, 2, 0)).astype(BF),
        'gqn': _row(g_qn), 'gqr': _pad_lanes(_row(g_qr)), 'gkn': _row(g_kn),
    }
    wuv = jnp.transpose(w_uv, (1, 0, 2)).astype(BF)
    wo2 = wo.reshape(-1, wo.shape[-1]).astype(BF)
    return w, wuv, wo2, rope


def _rope_tables(pos, rope):
    half = rope // 2
    inv_freq = ROPE_THETA ** (-jnp.arange(half, dtype=F32) / half)
    ang = pos.astype(F32)[:, None] * inv_freq[None, :]
    cos, sin = jnp.cos(ang), jnp.sin(ang)
    cos_t = _pad_lanes(jnp.concatenate([cos, cos], axis=1))
    sin_t = _pad_lanes(jnp.concatenate([-sin, sin], axis=1))
    return cos_t, sin_t


def _mla_layer(x, j, cos_t, sin_t, prm, cache_kv, cache_ks, page_table, dims):
    batch, seq, bd, ns = dims
    tp = batch * seq
    w, wuv, wo2, rope = prm
    h, kvr, _ = wuv.shape
    scale = float(w['wukt'].shape[1] + rope) ** -0.5
    qlat, qpe, ckv, kpe, ks = _mla_proj(x, cos_t, sin_t, w, bm=512)
    kv_row = jnp.concatenate([ckv, kpe[:, :rope]], axis=1)
    ks8 = ks[:, :h]
    c_bf = ckv[:tp].astype(BF)
    kpe_bf = kpe[:tp].astype(BF)
    kst = jnp.transpose(ks8[:tp])
    y_p = _flash_prompt(qlat, qpe, c_bf, kpe_bf, kst, x, wuv, wo2, batch=batch, seq=seq, tq=FLASH_TQ,
                        tk=FLASH_TK, hc=FLASH_HC, scale=scale)
    page = cache_kv.shape[3]
    qlat_s =jnp.transpose(qlat[:, tp:].reshape(h, bd, ns, kvr), (1, 0, 2, 3)).reshape(bd, h * ns, kvr)
    qpe_s = jnp.transpose(qpe[:, tp:].reshape(h, bd, ns, LANES), (1, 0, 2, 3)).reshape(bd, h * ns, LANES)
    newkv = jnp.pad(jnp.transpose(kv_row[tp:].reshape(bd, ns, kvr + rope), (0, 2, 1)),
                    ((0, 0), (0, 0), (0, page - ns)))
    newkst = jnp.pad(jnp.transpose(ks8[tp:].reshape(bd, ns, h), (0, 2, 1)), ((0, 0), (0, 0), (0, page - ns)))
    o_s = _decode_attend(page_table, qlat_s, qpe_s, cache_kv, j, cache_ks, newkv, newkst, pps=DECODE_PPS,
                         scale=scale)
    o_s = jnp.transpose(o_s.reshape(bd, h, ns, kvr), (0, 2, 1, 3)).reshape(bd * ns, h * kvr)
    y_s = _mla_out(o_s, x[tp:], wuv, wo2, bm=min(512, bd * ns))
    y = jnp.concatenate([y_p, y_s], axis=0)
    return y, kv_row, ks8


def _gmlp_layer(x, g_mix_l, w_in, g_v, w_s, b_s, w_out, dims):
    batch, seq, bd, ns = dims
    tp = batch * seq
    groups, chunk, _ = w_s.shape
    tri = jnp.tril(jnp.ones((chunk, chunk), bool))
    ws_p = jnp.where(tri[None], w_s, 0).astype(BF)
    bs_p = _pad_lanes(jnp.transpose(b_s))
    l = min(ns, chunk)
    tri_s = jnp.tril(jnp.ones((l, l), bool))
    ws_small = jnp.where(tri_s[None], w_s[:, :l, :l], 0)
    eye = jnp.eye(chunk // l, dtype=F32)
    ws_s = jnp.einsum('ab,gts->gatbs', eye, ws_small).reshape(groups, chunk, chunk).astype(BF)
    bs_s = _pad_lanes(jnp.tile(jnp.transpose(b_s[:, :l]), (chunk // l, 1)))
    args = (_row(g_mix_l), w_in.astype(BF), _row(g_v))
    wout = w_out.astype(BF)
    (y_p,) = _gmlp(x[:tp], *args, ws_p, bs_p, wout, bm=256, emit_v=False)
    y_s, v_s = _gmlp(x[tp:], *args, ws_s, bs_s, wout, bm=256, emit_v=True)
    return jnp.concatenate([y_p, y_s], axis=0), v_s


def _rglru_layer(x, g_mix_l, h0_s, buf_s, w_gate, w_x, conv_w, conv_b, w_a, b_a, w_i, b_i, lam, w_out, dims):
    batch, seq, bd, ns = dims
    tp = batch * seq
    nb, db, _ = w_a.shape
    ec = nb * db
    pair = 2
    gw = db * pair

    def blockdiag(wb):
        wb = wb.reshape(nb // pair, pair, db, db)
        eye = jnp.eye(pair, dtype=wb.dtype)
        return jnp.einsum('kpde,pq->kpdqe', wb, eye).reshape(nb // pair, gw, gw).astype(BF)

    w = {'wgate': w_gate.astype(BF), 'wx': w_x.astype(BF), 'cw': _pad_rows(conv_w), 'cb': _row(conv_b),
         'wa': blockdiag(w_a), 'ba': _row(b_a), 'wi': blockdiag(w_i), 'bi': _row(b_i),
         'sp': _row(jax.nn.softplus(-lam.astype(F32))), 'wout': w_out.astype(BF)}
    g = _row(g_mix_l)
    cw = conv_w.shape[0]
    zeros_h = jnp.zeros((SUBLANES, ec), F32)
    zeros_b = jnp.zeros((1, SUBLANES, ec), F32)
    y_p, hl_p, xl_p = _rglru(x[:tp], g, zeros_h, zeros_b, w, bt=256, batch=batch, seq=seq, segmented=False)
    buf8 = jnp.pad(buf_s.astype(F32), ((0, 0), (SUBLANES - (cw - 1), 0), (0, 0)))
    y_s, hl_s, xl_s = _rglru(x[tp:], g, h0_s.astype(F32), buf8, w, bt=256, batch=bd, seq=ns, segmented=True)
    h_p = hl_p[:, 0, :]
    conv_p = xl_p[:, SUBLANES - (cw - 1):, :]
    conv_s = xl_s.reshape(bd, ns, ec)[:, ns - (cw - 1):, :]
    return jnp.concatenate([y_p, y_s], axis=0), h_p, conv_p, hl_s, conv_s


def _pad_rows(v, rows=SUBLANES):
    return jnp.pad(v.astype(F32), ((0, rows - v.shape[0]), (0, 0)))


def _moe_layer(x, g_ffn_l, router, w1, w3, w2, *, bm=512, bf=1792):
    t, d = x.shape
    ne = router.shape[1]
    f, info, cnt = _router(x, _row(g_ffn_l), router, bm=512)
    counts = cnt[0, :ne].astype(jnp.int32)
    tiles_per = (counts + bm - 1) // bm
    tile_start = jnp.cumsum(tiles_per) - tiles_per
    n_tiles = (t * TOP_K) // bm + ne
    r_pad = n_tiles * bm
    tile_ids = jnp.arange(n_tiles)
    n_used = jnp.sum(tiles_per)
    tile_expert = jnp.sum((tile_ids[:, None] >= tile_start[None, :]).astype(jnp.int32), axis=1) - 1
    last_e = jnp.max(jnp.where(tiles_per > 0, jnp.arange(ne), 0))
    tile_expert = jnp.where(tile_ids < n_used, tile_expert, last_e).astype(jnp.int32)
    base = (tile_start * bm).astype(jnp.int32)
    sel = info[:, :TOP_K].astype(jnp.int32)
    dest = jnp.take(base, sel) + info[:, TOP_K:2 * TOP_K].astype(jnp.int32)
    npc = d // SC_WORDS
    off = (jnp.arange(npc, dtype=jnp.int32) * r_pad)[:, None]
    idx_a = (off + dest[None, :, 0]).reshape(-1)
    idx_b = (off + dest[None, :, 1]).reshape(-1)
    xs = _sc_dispatch(f.reshape(npc * t, SC_WORDS), idx_a, idx_b, npc * r_pad).reshape(npc, r_pad, SC_WORDS)
    ys = _ffn_grouped(tile_expert, n_used.reshape(1).astype(jnp.int32), xs,
                      w1.astype(BF), w3.astype(BF), w2.astype(BF), bm=bm, bf=bf)
    ys_p = ys.reshape(npc * r_pad, SC_WORDS)
    ya = _sc_gather(ys_p, idx_a).reshape(npc, t, SC_WORDS)
    yb = _sc_gather(ys_p, idx_b).reshape(npc, t, SC_WORDS)
    return _moe_combine(x, ya, yb, info, bm=512)


def kernel(x_prompt, x_sample, cache_mla_kv, cache_mla_kscale, state_rglru_h, state_rglru_conv, page_table, g_mix, g_ffn, mla_wq_a, mla_g_qa, mla_wq_b, mla_wkv_a, mla_g_kva, mla_w_uk, mla_w_uv, mla_wo, mla_g_qn, mla_g_qr, mla_g_kn, mla_g_kr, gm_w_in, gm_g_v, gm_w_s, gm_b_s, gm_w_out, rg_w_gate, rg_w_x, rg_conv_w, rg_conv_b, rg_w_a, rg_b_a, rg_w_i, rg_b_i, rg_lam, rg_w_out, ffd_w1, ffd_w3, ffd_w2, moe_router, moe_w1, moe_w3, moe_w2):
    batch, seq, d = x_prompt.shape
    bd, ns, _ = x_sample.shape
    dims = (batch, seq, bd, ns)
    tp = batch * seq
    depth = g_mix.shape[0]
    n_mixers = 3
    past_len = page_table.shape[1] * cache_mla_kv.shape[2]
    cache_kv_t = jnp.swapaxes(cache_mla_kv, 2, 3)
    cache_ks_t = jnp.swapaxes(cache_mla_kscale, 2, 3)

    x = jnp.concatenate([x_prompt.reshape(tp, d), x_sample.reshape(bd * ns, d)], axis=0)
    rope = mla_wq_b.shape[-1] - mla_w_uk.shape[-1]
    pos = jnp.concatenate([jnp.tile(jnp.arange(seq, dtype=jnp.int32), batch),
                           jnp.tile(past_len + jnp.arange(ns, dtype=jnp.int32), bd)])
    cos_t, sin_t = _rope_tables(pos, rope)

    kv_out, ks_out, v_out, hp_out, cp_out, hs_out, cs_out = [], [], [], [], [], [], []
    counts = [0, 0, 0]
    for layer in range(depth):
        kind = layer % n_mixers
        j = counts[kind]
        counts[kind] += 1
        if kind == 0:
            prm = _prep_mla(g_mix[layer], mla_wq_a[j], mla_g_qa[j], mla_wq_b[j], mla_wkv_a[j], mla_g_kva[j],
                            mla_w_uk[j], mla_w_uv[j], mla_wo[j], mla_g_qn[j], mla_g_qr[j], mla_g_kn[j],
                            mla_g_kr[j])
            x, kv_row, ks8 = _mla_layer(x, j, cos_t, sin_t, prm, cache_kv_t, cache_ks_t, page_table, dims)
            kv_out.append(kv_row)
            ks_out.append(ks8)
        elif kind == 1:
            x, v_s = _gmlp_layer(x, g_mix[layer], gm_w_in[j], gm_g_v[j], gm_w_s[j], gm_b_s[j], gm_w_out[j], dims)
            v_out.append(v_s.reshape(bd, ns, -1))
        else:
            x, h_p, c_p, h_s, c_s = _rglru_layer(
                x, g_mix[layer], state_rglru_h[j], state_rglru_conv[j], rg_w_gate[j], rg_w_x[j], rg_conv_w[j],
                rg_conv_b[j], rg_w_a[j], rg_b_a[j], rg_w_i[j], rg_b_i[j], rg_lam[j], rg_w_out[j], dims)
            hp_out.append(h_p)
            cp_out.append(c_p)
            hs_out.append(h_s)
            cs_out.append(c_s)
        i = layer // 2
        if layer % 2 == 0:
            x = _ffn_dense(x, _row(g_ffn[layer]), ffd_w1[i].astype(BF), ffd_w3[i].astype(BF),
                           ffd_w2[i].astype(BF), bm=512, bf=1408)
        else:
            x = _moe_layer(x, g_ffn[layer], moe_router[i], moe_w1[i], moe_w3[i], moe_w2[i])

    kv_all = jnp.stack(kv_out)
    ks_all = jnp.stack(ks_out)
    return (x[:tp].reshape(batch, seq, d), x[tp:].reshape(bd, ns, d),
            kv_all[:, :tp].reshape(len(kv_out), batch, seq, -1), ks_all[:, :tp].reshape(len(ks_out), batch, seq, -1),
            kv_all[:, tp:].reshape(len(kv_out), bd, ns, -1), ks_all[:, tp:].reshape(len(ks_out), bd, ns, -1),
            jnp.stack(v_out), jnp.stack(hp_out), jnp.stack(cp_out), jnp.stack(hs_out), jnp.stack(cs_out))
```

```python
import functools

import jax
import jax.numpy as jnp
from jax import lax
from jax.experimental import pallas as pl
from jax.experimental.pallas import tpu as pltpu
from jax.experimental.pallas import tpu_sc as plsc

BF = jnp.bfloat16
F32 = jnp.float32
EPS = 1e-6
ROPE_THETA = 10000.0
C_RG = 8.0
TOP_K = 2
NEG_BIG = -1e30
LOG2E = 1.4426950408889634

LANES = 128
SUBLANES = 8
VMEM_LIMIT = 56 * 1024 * 1024

FLASH_TQ = 128
FLASH_TK = 2048
FLASH_HC = 2
DECODE_PPS = 64


def _cparams(sem):
    return pltpu.CompilerParams(dimension_semantics=sem, vmem_limit_bytes=VMEM_LIMIT)


def _dot(a, b):
    return jnp.dot(a, b, preferred_element_type=F32)


def _dot_nt(a, b):
    return lax.dot_general(a, b, (((1,), (1,)), ((), ())), preferred_element_type=F32)


def _rms(x):
    return x * lax.rsqrt(jnp.mean(x * x, axis=-1, keepdims=True) + EPS)


def _rms_half(x):
    return x * lax.rsqrt(jnp.sum(x * x, axis=-1, keepdims=True) * (1.0 / 64.0) + EPS)


def _rope_half(x, cos, sin):
    lane = lax.broadcasted_iota(jnp.int32, x.shape, 1)
    rot = jnp.where(lane < 32, pltpu.roll(x, 96, 1), pltpu.roll(x, 32, 1))
    return x * cos + rot * sin


def _gelu(x):
    return 0.5 * x * (1.0 + jnp.tanh(0.7978845608028654 * (x + 0.044715 * (x * x * x))))


def _sigmoid(x):
    return 1.0 / (1.0 + jnp.exp(-x))


def _full(shape):
    nd = len(shape)
    return pl.BlockSpec(shape, lambda *_: (0,) * nd)


def _mla_proj_body(x_ref, cos_ref, sin_ref, gmix_ref, wqa_ref, gqa_ref, wqn_ref, wqr_ref,
                   wkvc_ref, wkvr_ref, gkva_ref, gkr_ref, wukf_ref, wukt_ref, gqn_ref, gqr_ref,
                   gkn_ref, qlat_ref, qpe_ref, ckv_ref, kpe_ref, ks_ref, *, n_heads):
    x = x_ref[...]
    n = (_rms(x) * gmix_ref[...]).astype(BF)
    cos = cos_ref[...]
    sin = sin_ref[...]
    cq = (_rms(_dot(n, wqa_ref[...])) * gqa_ref[...]).astype(BF)
    ckv = _rms(_dot(n, wkvc_ref[...])) * gkva_ref[...]
    ckv_ref[...] = ckv
    kr = _dot(n, wkvr_ref[...])
    kpe_ref[...] = _rope_half(_rms_half(kr) * gkr_ref[...], cos, sin)
    kn = _dot(ckv.astype(BF), wukf_ref[...])
    lane = lax.broadcasted_iota(jnp.int32, (x.shape[0], LANES), 1)
    ks = jnp.zeros((x.shape[0], LANES), F32)
    for h in range(n_heads):
        blk = kn[:, h * 128:(h + 1) * 128]
        ksh = lax.rsqrt(jnp.mean(blk * blk, axis=-1, keepdims=True) + EPS)
        ks = jnp.where(lane == h, ksh, ks)
    ks_ref[...] = ks
    qn = _dot(cq, wqn_ref[...])
    qr = _dot(cq, wqr_ref[...])
    gq = gqn_ref[...]
    gk = gkn_ref[...]
    for h in range(n_heads):
        a = (_rms(qn[:, h * 128:(h + 1) * 128]) * gq) * gk
        qlat_ref[h] = _dot(a.astype(BF), wukt_ref[h]).astype(BF)
        r = _rms_half(qr[:, h * 128:(h + 1) * 128]) * gqr_ref[...]
        qpe_ref[h] = _rope_half(r, cos, sin).astype(BF)


def _mla_proj(x, cos, sin, w, *, bm):
    t, d = x.shape
    h = w['wukt'].shape[0]
    kvr = w['wkvc'].shape[1]
    row = lambda i: (i, 0)
    in_specs = [pl.BlockSpec((bm, d), row), pl.BlockSpec((bm, LANES), row), pl.BlockSpec((bm, LANES), row)]
    names = ['gmix', 'wqa', 'gqa', 'wqn', 'wqr', 'wkvc', 'wkvr', 'gkva', 'gkr', 'wukf', 'wukt', 'gqn', 'gqr', 'gkn']
    in_specs += [_full(w[k].shape) for k in names]
    out_shape = (jax.ShapeDtypeStruct((h, t, kvr), BF), jax.ShapeDtypeStruct((h, t, LANES), BF),
                 jax.ShapeDtypeStruct((t, kvr), F32), jax.ShapeDtypeStruct((t, LANES), F32),
                 jax.ShapeDtypeStruct((t, LANES), F32))
    out_specs = (pl.BlockSpec((h, bm, kvr), lambda i: (0, i, 0)), pl.BlockSpec((h, bm, LANES), lambda i: (0, i, 0)),
                 pl.BlockSpec((bm, kvr), row), pl.BlockSpec((bm, LANES), row), pl.BlockSpec((bm, LANES), row))
    return pl.pallas_call(
        functools.partial(_mla_proj_body, n_heads=h),
        grid=(t // bm,), in_specs=in_specs, out_specs=out_specs, out_shape=out_shape,
        compiler_params=_cparams(("parallel",)), name="mla_proj",
    )(x, cos, sin, *[w[k] for k in names])


def _mla_out_rows(o_heads, wuv_ref, wo_ref, obuf_ref):
    for h, oh in enumerate(o_heads):
        v = _dot(oh.astype(BF), wuv_ref[h])
        obuf_ref[:, h * 128:(h + 1) * 128] = v.astype(BF)
    return _dot(obuf_ref[...], wo_ref[...])


def _mla_out_body(o_ref, x_ref, wuv_ref, wo_ref, y_ref, obuf_ref, *, n_heads, kvr):
    o = o_ref[...]
    heads = [o[:, h * kvr:(h + 1) * kvr] for h in range(n_heads)]
    y_ref[...] = x_ref[...] + _mla_out_rows(heads, wuv_ref, wo_ref, obuf_ref)


def _mla_out(o, x, wuv, wo, *, bm):
    t, d = x.shape
    h, kvr, vd = wuv.shape
    return pl.pallas_call(
        functools.partial(_mla_out_body, n_heads=h, kvr=kvr),
        grid=(t // bm,),
        in_specs=[pl.BlockSpec((bm, h * kvr), lambda i: (i, 0)), pl.BlockSpec((bm, d), lambda i: (i, 0)),
                  _full(wuv.shape), _full(wo.shape)],
        out_specs=pl.BlockSpec((bm, d), lambda i: (i, 0)),
        out_shape=jax.ShapeDtypeStruct((t, d), F32),
        scratch_shapes=[pltpu.VMEM((bm, h * vd), BF)],
        compiler_params=_cparams(("parallel",)), name="mla_out",
    )(o, x, wuv, wo)


def _flash_body(qi_ref, kj_ref, qlat_ref, qpe_ref, c_ref, kpe_ref, kst_ref, x_ref, wuv_ref, wo_ref, y_ref,
                m_sc, l_sc, acc_sc, obuf_ref, *, tq, tk, n_heads, hc, c2):
    pidx = pl.program_id(1)
    i = qi_ref[pidx]
    j = kj_ref[pidx]
    last_needed = (i * tq + tq - 1) // tk
    kvr = qlat_ref.shape[-1]

    @pl.when(j == 0)
    def _():
        m_sc[...] = jnp.full(m_sc.shape, NEG_BIG, F32)
        l_sc[...] = jnp.zeros(l_sc.shape, F32)
        acc_sc[...] = jnp.zeros(acc_sc.shape, F32)

    def step(masked):
        c = c_ref[...]
        kpe = kpe_ref[...]
        kst = kst_ref[...] * c2
        if masked:
            qpos = i * tq + lax.broadcasted_iota(jnp.int32, (tq, tk), 0)
            kpos = j * tk + lax.broadcasted_iota(jnp.int32, (tq, tk), 1)
            keep = kpos <= qpos
        for g in range(n_heads // hc):
            r0, nr = g * hc * tq, hc * tq
            q = qlat_ref[g * hc:(g + 1) * hc].reshape(nr, kvr)
            qp = qpe_ref[g * hc:(g + 1) * hc].reshape(nr, LANES)
            s_n = _dot_nt(q, c)
            s_p = _dot_nt(qp, kpe)
            p_rows = []
            for hh in range(hc):
                h = g * hc + hh
                hr = slice(hh * tq, (hh + 1) * tq)
                pieces = []
                mx = None
                for kt in range(tk // LANES):
                    cols = slice(kt * LANES, (kt + 1) * LANES)
                    sh = s_n[hr, cols] * kst[h:h + 1, cols] + s_p[hr, cols] * c2
                    if masked:
                        sh = jnp.where(keep[:, cols], sh, NEG_BIG)
                    pieces.append(sh)
                    mx = sh if mx is None else jnp.maximum(mx, sh)
                hs = slice(h * tq, (h + 1) * tq)
                m_prev = m_sc[hs]
                m_cur = jnp.maximum(m_prev, jnp.max(mx, axis=1, keepdims=True))
                alpha = jnp.exp2(m_prev - m_cur)
                ps = []
                sm = None
                for sh in pieces:
                    p = jnp.exp2(sh - m_cur)
                    sm = p if sm is None else sm + p
                    ps.append(p.astype(BF))
                l_sc[hs] = alpha * l_sc[hs] + jnp.sum(sm, axis=1, keepdims=True)
                m_sc[hs] = m_cur
                acc_sc[hs] = alpha * acc_sc[hs]
                p_rows.append(jnp.concatenate(ps, axis=1))
            p_all = jnp.concatenate(p_rows, axis=0) if hc > 1 else p_rows[0]
            acc_sc[r0:r0 + nr] += _dot(p_all, c)

    @pl.when(j < last_needed)
    def _():
        step(False)

    @pl.when(j == last_needed)
    def _():
        step(True)
        inv = 1.0 / l_sc[...]
        heads = [acc_sc[h * tq:(h + 1) * tq, :] * inv[h * tq:(h + 1) * tq] for h in range(n_heads)]
        y_ref[...] = x_ref[...] + _mla_out_rows(heads, wuv_ref, wo_ref, obuf_ref)


def _flash_prompt(qlat, qpe, c_bf, kpe_bf, kst, x, wuv, wo, *, batch, seq, tq, tk, hc, scale):
    h, _, kvr = qlat.shape
    d = x.shape[1]
    nq = seq // tq
    nk = seq // tk
    pairs = [(i, j) for i in range(nq) for j in range((i * tq + tq - 1) // tk + 1)]
    qi = jnp.asarray([p[0] for p in pairs], jnp.int32)
    kj = jnp.asarray([p[1] for p in pairs], jnp.int32)
    in_specs = [
        pl.BlockSpec((h, tq, kvr), lambda b, p, qi, kj: (0, b * nq + qi[p], 0)),
        pl.BlockSpec((h, tq, LANES), lambda b, p, qi, kj: (0, b * nq + qi[p], 0)),
        pl.BlockSpec((tk, kvr), lambda b, p, qi, kj: (b * nk + kj[p], 0)),
        pl.BlockSpec((tk, LANES), lambda b, p, qi, kj: (b * nk + kj[p], 0)),
        pl.BlockSpec((h, tk), lambda b, p, qi, kj: (0, b * nk + kj[p])),
        pl.BlockSpec((tq, d), lambda b, p, qi, kj: (b * nq + qi[p], 0)),
        pl.BlockSpec(wuv.shape, lambda b, p, qi, kj: (0, 0, 0)),
        pl.BlockSpec(wo.shape, lambda b, p, qi, kj: (0, 0)),
    ]
    grid_spec = pltpu.PrefetchScalarGridSpec(
        num_scalar_prefetch=2, grid=(batch, len(pairs)), in_specs=in_specs,
        out_specs=pl.BlockSpec((tq, d), lambda b, p, qi, kj: (b * nq + qi[p], 0)),
        scratch_shapes=[pltpu.VMEM((h * tq, 1), F32), pltpu.VMEM((h * tq, 1), F32),
                        pltpu.VMEM((h * tq, kvr), F32), pltpu.VMEM((tq, wo.shape[0]), BF)])
    return pl.pallas_call(
        functools.partial(_flash_body, tq=tq, tk=tk, n_heads=h, hc=hc, c2=scale * LOG2E),
        grid_spec=grid_spec, out_shape=jax.ShapeDtypeStruct((batch * seq, d), F32),
        compiler_params=_cparams(("parallel", "arbitrary")), name="mla_flash_prompt",
    )(qi, kj, qlat, qpe, c_bf, kpe_bf, kst, x, wuv, wo)


def _decode_body(pt_ref, q_ref, qp_ref, *rest, pps, page, n_heads, n_new, kvr, rope, c2):
    kv_refs = rest[:pps]
    ks_refs = rest[pps:2 * pps]
    newkv_ref, newks_ref, o_ref, kbuf, kst_sc, m_sc, l_sc, acc_sc = rest[2 * pps:]
    cidx = pl.program_id(1)
    nc = pl.num_programs(1)
    rows = n_heads * n_new

    @pl.when(cidx == 0)
    def _():
        m_sc[...] = jnp.full(m_sc.shape, NEG_BIG, F32)
        l_sc[...] = jnp.zeros(l_sc.shape, F32)
        acc_sc[...] = jnp.zeros(acc_sc.shape, F32)

    q = q_ref[...]
    qp = qp_ref[:, :rope]

    def attend(k_all, kst, keep):
        ct = k_all[:kvr]
        s_n = _dot(q, ct)
        s_p = _dot(qp, k_all[kvr:kvr + rope])
        kst = kst * c2
        parts = []
        for h in range(n_heads):
            sl = slice(h * n_new, (h + 1) * n_new)
            parts.append(s_n[sl] * kst[h:h + 1, :] + s_p[sl] * c2)
        s = jnp.concatenate(parts, axis=0)
        if keep is not None:
            s = jnp.where(keep, s, NEG_BIG)
        m_prev = m_sc[...]
        m_cur = jnp.maximum(m_prev, jnp.max(s, axis=1, keepdims=True))
        alpha = jnp.exp2(m_prev - m_cur)
        p = jnp.exp2(s - m_cur)
        l_sc[...] = alpha * l_sc[...] + jnp.sum(p, axis=1, keepdims=True)
        acc_sc[...] = alpha * acc_sc[...] + _dot_nt(p.astype(BF), ct)
        m_sc[...] = m_cur

    for p in range(pps):
        kbuf[:, p * page:(p + 1) * page] = kv_refs[p][...].astype(BF)
        kst_sc[:, p * page:(p + 1) * page] = ks_refs[p][...]
    attend(kbuf[...], kst_sc[...], None)

    @pl.when(cidx == nc - 1)
    def _():
        r = lax.broadcasted_iota(jnp.int32, (rows, page), 0)
        kj = lax.broadcasted_iota(jnp.int32, (rows, page), 1)
        keep = kj <= (r % n_new)
        attend(newkv_ref[...].astype(BF), newks_ref[...], keep)
        o_ref[...] = acc_sc[...] * (1.0 / l_sc[...])


def _decode_attend(page_table, qlat_s, qpe_s, cache_kv, j, cache_ks, newkv, newkst, *, pps, scale):
    bd, n_pages = page_table.shape
    rows, kvr = qlat_s.shape[1:]
    width, page = cache_kv.shape[2:]
    n_heads = cache_ks.shape[2]
    n_new = rows // n_heads
    rope = width - kvr
    nc = n_pages // pps
    pt_flat = page_table.reshape(-1)

    def page_map_kv(p):
        return lambda b, c, pt: (j, pt[b * n_pages + c * pps + p], 0, 0)

    in_specs = [pl.BlockSpec((None, rows, kvr), lambda b, c, pt: (b, 0, 0)),
                pl.BlockSpec((None, rows, LANES), lambda b, c, pt: (b, 0, 0))]
    in_specs += [pl.BlockSpec((None, None, width, page), page_map_kv(p)) for p in range(pps)]
    in_specs += [pl.BlockSpec((None, None, n_heads, page), page_map_kv(p)) for p in range(pps)]
    in_specs += [pl.BlockSpec((None, width, page), lambda b, c, pt: (b, 0, 0)),
                 pl.BlockSpec((None, n_heads, page), lambda b, c, pt: (b, 0, 0))]
    grid_spec = pltpu.PrefetchScalarGridSpec(
        num_scalar_prefetch=1, grid=(bd, nc), in_specs=in_specs,
        out_specs=pl.BlockSpec((None, rows, kvr), lambda b, c, pt: (b, 0, 0)),
        scratch_shapes=[pltpu.VMEM((width, pps * page), BF), pltpu.VMEM((n_heads, pps * page), F32),
                        pltpu.VMEM((rows, 1), F32), pltpu.VMEM((rows, 1), F32), pltpu.VMEM((rows, kvr), F32)])
    return pl.pallas_call(
        functools.partial(_decode_body, pps=pps, page=page, n_heads=n_heads, n_new=n_new, kvr=kvr,
                          rope=rope, c2=scale * LOG2E),
        grid_spec=grid_spec, out_shape=jax.ShapeDtypeStruct((bd, rows, kvr), F32),
        compiler_params=_cparams(("parallel", "arbitrary")), name="mla_decode",
    )(pt_flat, qlat_s, qpe_s, *([cache_kv] * pps), *([cache_ks] * pps), newkv, newkst)


def _ffn_dense_body(x_ref, g_ref, w1_ref, w3_ref, w2_ref, o_ref, h_sc):
    j = pl.program_id(1)

    @pl.when(j == 0)
    def _():
        x = x_ref[...]
        h_sc[...] = (_rms(x) * g_ref[...]).astype(BF)
        o_ref[...] = x

    h = h_sc[...]
    a = _dot(h, w1_ref[...])
    b = _dot(h, w3_ref[...])
    g = (a * _sigmoid(a)) * b
    o_ref[...] += _dot(g.astype(BF), w2_ref[...])


def _ffn_dense(x, g, w1, w3, w2, *, bm, bf):
    t, d = x.shape
    ff = w1.shape[1]
    return pl.pallas_call(
        _ffn_dense_body, grid=(t // bm, ff // bf),
        in_specs=[pl.BlockSpec((bm, d), lambda i, j: (i, 0)), _full(g.shape),
                  pl.BlockSpec((d, bf), lambda i, j: (0, j)), pl.BlockSpec((d, bf), lambda i, j: (0, j)),
                  pl.BlockSpec((bf, d), lambda i, j: (j, 0))],
        out_specs=pl.BlockSpec((bm, d), lambda i, j: (i, 0)),
        out_shape=jax.ShapeDtypeStruct((t, d), F32),
        scratch_shapes=[pltpu.VMEM((bm, d), BF)],
        compiler_params=_cparams(("parallel", "arbitrary")), name="ffn_dense",
    )(x, g, w1, w3, w2)


def _ffn_group_body(te_ref, nt_ref, x_ref, w1_ref, w3_ref, w2_ref, o_ref, h_sc):
    i = pl.program_id(0)
    j = pl.program_id(1)

    @pl.when(i < nt_ref[0])
    def _():
        npc, _, w = x_ref.shape

        @pl.when(j == 0)
        def _():
            for c in range(npc):
                h_sc[:, c * w:(c + 1) * w] = x_ref[c].astype(BF)
            o_ref[...] = jnp.zeros(o_ref.shape, F32)

        h = h_sc[...]
        a = _dot(h, w1_ref[...])
        b = _dot(h, w3_ref[...])
        g = (a * _sigmoid(a)) * b
        y = _dot(g.astype(BF), w2_ref[...])
        for c in range(npc):
            o_ref[c] += y[:, c * w:(c + 1) * w]


def _ffn_grouped(tile_expert, n_tiles_used, xs, w1, w3, w2, layer, *, bm, bf):
    npc, r, w = xs.shape
    d = npc * w
    ff = w1.shape[3]
    grid_spec = pltpu.PrefetchScalarGridSpec(
        num_scalar_prefetch=2, grid=(r // bm, ff // bf),
        in_specs=[pl.BlockSpec((npc, bm, w), lambda i, j, te, nt: (0, i, 0)),
                  pl.BlockSpec((None, None, d, bf), lambda i, j, te, nt: (layer, te[i], 0, j)),
                  pl.BlockSpec((None, None, d, bf), lambda i, j, te, nt: (layer, te[i], 0, j)),
                  pl.BlockSpec((None, None, bf, d), lambda i, j, te, nt: (layer, te[i], j, 0))],
        out_specs=pl.BlockSpec((npc, bm, w), lambda i, j, te, nt: (0, i, 0)),
        scratch_shapes=[pltpu.VMEM((bm, d), BF)])
    return pl.pallas_call(
        _ffn_group_body, grid_spec=grid_spec, out_shape=jax.ShapeDtypeStruct((npc, r, w), F32),
        compiler_params=_cparams(("parallel", "arbitrary")), name="ffn_grouped",
    )(tile_expert, n_tiles_used, xs, w1, w3, w2)


def _router_body(x_ref, g_ref, wr_ref, tri_ref, f_ref, info_ref, cnt_ref, carry_sc, *, n_experts):
    @pl.when(pl.program_id(0) == 0)
    def _():
        carry_sc[...] = jnp.zeros(carry_sc.shape, F32)

    f = _rms(x_ref[...]) * g_ref[...]
    for c in range(f_ref.shape[0]):
        f_ref[c] = f[:, c * SC_WORDS:(c + 1) * SC_WORDS]
    lane = lax.broadcasted_iota(jnp.int32, (f.shape[0], LANES), 1)
    f_hi = f.astype(BF)
    f_lo = (f - f_hi.astype(F32)).astype(BF)
    raw = _dot(f_hi, wr_ref[0]) + (_dot(f_lo, wr_ref[0]) + _dot(f_hi, wr_ref[1]))
    logits = jnp.where(lane < n_experts, raw, -jnp.inf)
    v1 = jnp.max(logits, axis=1, keepdims=True)
    i1 = jnp.min(jnp.where(logits == v1, lane, LANES), axis=1, keepdims=True)
    rest = jnp.where(lane == i1, -jnp.inf, logits)
    v2 = jnp.max(rest, axis=1, keepdims=True)
    i2 = jnp.min(jnp.where(rest == v2, lane, LANES), axis=1, keepdims=True)
    e2 = jnp.exp(v2 - v1)
    den = 1.0 + e2
    g1 = 1.0 / den
    g2 = e2 / den
    hot1 = lane == i1
    hot2 = lane == i2
    onehot = jnp.where(hot1 | hot2, 1.0, 0.0)
    rank = _dot(tri_ref[...], onehot.astype(BF)) + carry_sc[...]
    carry = carry_sc[...] + jnp.sum(onehot, axis=0, keepdims=True)
    carry_sc[...] = carry
    cnt_ref[...] = jnp.broadcast_to(carry, cnt_ref.shape)
    r1 = jnp.sum(jnp.where(hot1, rank, 0.0), axis=1, keepdims=True)
    r2 = jnp.sum(jnp.where(hot2, rank, 0.0), axis=1, keepdims=True)
    info = jnp.zeros((f.shape[0], LANES), F32)
    for k, col in enumerate((i1.astype(F32), i2.astype(F32), r1, r2, g1, g2)):
        info = jnp.where(lane == k, col, info)
    info_ref[...] = info


def _router(x, g, wr, *, bm):
    t, d = x.shape
    n_experts = wr.shape[1]
    wr_f = _pad_lanes(wr.astype(F32))
    wr_hi = wr_f.astype(BF)
    wr_p = jnp.stack([wr_hi, (wr_f - wr_hi.astype(F32)).astype(BF)])
    tri =(jnp.arange(bm)[:, None] > jnp.arange(bm)[None, :]).astype(BF)
    return pl.pallas_call(
        functools.partial(_router_body, n_experts=n_experts), grid=(t // bm,),
        in_specs=[pl.BlockSpec((bm, d), lambda i: (i, 0)), _full(g.shape), _full(wr_p.shape), _full(tri.shape)],
        out_specs=(pl.BlockSpec((d // SC_WORDS, bm, SC_WORDS), lambda i: (0, i, 0)),
                   pl.BlockSpec((bm, LANES), lambda i: (i, 0)),
                   pl.BlockSpec((SUBLANES, LANES), lambda i: (0, 0))),
        out_shape=(jax.ShapeDtypeStruct((d // SC_WORDS, t, SC_WORDS), F32), jax.ShapeDtypeStruct((t, LANES), F32),
                   jax.ShapeDtypeStruct((SUBLANES, LANES), F32)),
        scratch_shapes=[pltpu.VMEM((1, LANES), F32)],
        compiler_params=_cparams(("arbitrary",)), name="moe_router",
    )(x, g, wr_p, tri)


SC_WORDS = 256
SC_WINDOW = 128


def _sc_mesh():
    return plsc.VectorSubcoreMesh(core_axis_name="c", subcore_axis_name="s")


def _sc_dispatch(pieces, idx_a, idx_b, n_out):
    n, w = pieces.shape

    @pl.kernel(out_type=jax.ShapeDtypeStruct((n_out, w), pieces.dtype), mesh=_sc_mesh())
    def k(x_hbm, ia_hbm, ib_hbm, o_hbm):
        def body(x_vmem, ia_vmem, ib_vmem):
            pltpu.sync_copy(x_vmem, o_hbm.at[ia_vmem.at[0]])
            pltpu.sync_copy(x_vmem, o_hbm.at[ib_vmem.at[0]])

        pltpu.emit_pipeline(
            body, grid=(n // SC_WINDOW,),
            in_specs=[pl.BlockSpec((SC_WINDOW, w), lambda i: (i, 0)),
                      pl.BlockSpec((1, SC_WINDOW), lambda i: (0, i)),
                      pl.BlockSpec((1, SC_WINDOW), lambda i: (0, i))],
            out_specs=[], core_axis_name=('c', 's'), dimension_semantics=(pltpu.PARALLEL,),
        )(x_hbm, ia_hbm, ib_hbm)

    return k(pieces, idx_a.reshape(1, n), idx_b.reshape(1, n))


def _sc_gather(table, idx):
    n = idx.shape[0]
    w = table.shape[1]

    @pl.kernel(out_type=jax.ShapeDtypeStruct((n, w), table.dtype), mesh=_sc_mesh())
    def k(t_hbm, i_hbm, o_hbm):
        def body(i_vmem, o_vmem):
            pltpu.sync_copy(t_hbm.at[i_vmem.at[0]], o_vmem)

        pltpu.emit_pipeline(
            body, grid=(n // SC_WINDOW,),
            in_specs=[pl.BlockSpec((1, SC_WINDOW), lambda i: (0, i))],
            out_specs=[pl.BlockSpec((SC_WINDOW, w), lambda i: (i, 0))],
            core_axis_name=('c', 's'), dimension_semantics=(pltpu.PARALLEL,),
        )(i_hbm, o_hbm)

    return k(table, idx.reshape(1, n))


def _combine_body(x_ref, ya_ref, yb_ref, info_ref, o_ref):
    info = info_ref[...]
    ga, gb = info[:, 4:5], info[:, 5:6]
    npc, _, w = ya_ref.shape
    for c in range(npc):
        cols = slice(c * w, (c + 1) * w)
        o_ref[:, cols] = x_ref[:, cols] + (ga * ya_ref[c] + gb * yb_ref[c])


def _moe_combine(x, ya, yb, info, *, bm):
    t, d = x.shape
    npc, _, w = ya.shape
    row = lambda i: (i, 0)
    blk = pl.BlockSpec((npc, bm, w), lambda i: (0, i, 0))
    return pl.pallas_call(
        _combine_body, grid=(t // bm,),
        in_specs=[pl.BlockSpec((bm, d), row), blk, blk, pl.BlockSpec((bm, LANES), row)],
        out_specs=pl.BlockSpec((bm, d), row), out_shape=jax.ShapeDtypeStruct((t, d), F32),
        compiler_params=_cparams(("parallel",)), name="moe_combine",
    )(x, ya, yb, info)


def _gmlp_body(x_ref, g_ref, win_ref, gv_ref, ws_ref, bs_ref, wout_ref, y_ref, *maybe_v, chunk, groups):
    x = x_ref[...]
    bm = x.shape[0]
    n = (_rms(x) * g_ref[...]).astype(BF)
    z = _gelu(_dot(n, win_ref[...]))
    e = z.shape[1] // 2
    dg = e // groups
    u = z[:, :e]
    v = _rms(z[:, e:]) * gv_ref[...]
    if maybe_v:
        maybe_v[0][...] = v
    vb = v.astype(BF)
    bs = bs_ref[...]
    rows = []
    for c in range(bm // chunk):
        cols = []
        for g in range(groups):
            mixed = _dot(ws_ref[g], vb[c * chunk:(c + 1) * chunk, g * dg:(g + 1) * dg]) + bs[:, g:g + 1]
            cols.append(u[c * chunk:(c + 1) * chunk, g * dg:(g + 1) * dg] * mixed)
        rows.append(jnp.concatenate(cols, axis=1))
    y = jnp.concatenate(rows, axis=0) if len(rows) > 1 else rows[0]
    y_ref[...] = x + _dot(y.astype(BF), wout_ref[...])


def _gmlp(x, g, win, gv, ws, bs, wout, *, bm, emit_v):
    t, d = x.shape
    e = wout.shape[0]
    groups, chunk, _ = ws.shape
    out_shape = [jax.ShapeDtypeStruct((t, d), F32)]
    out_specs = [pl.BlockSpec((bm, d), lambda i: (i, 0))]
    if emit_v:
        out_shape.append(jax.ShapeDtypeStruct((t, e), F32))
        out_specs.append(pl.BlockSpec((bm, e), lambda i: (i, 0)))
    res = pl.pallas_call(
        functools.partial(_gmlp_body, chunk=chunk, groups=groups),
        grid=(t // bm,),
        in_specs=[pl.BlockSpec((bm, d), lambda i: (i, 0)), _full(g.shape), _full(win.shape), _full(gv.shape),
                  _full(ws.shape), _full(bs.shape), _full(wout.shape)],
        out_specs=tuple(out_specs), out_shape=tuple(out_shape),
        compiler_params=_cparams(("parallel",)), name="gmlp_v" if emit_v else "gmlp",
    )(x, g, win, gv, ws, bs, wout)
    return res


def _scan8(a8, u8):
    row = lax.broadcasted_iota(jnp.int32, a8.shape, 0)
    A, B = a8, u8
    for d in (1, 2, 4):
        a_sh = pltpu.roll(A, d, 0)
        b_sh = pltpu.roll(B, d, 0)
        m = row >= d
        B = jnp.where(m, A * b_sh + B, B)
        A = jnp.where(m, A * a_sh, A)
    return A, B


def _rglru_body(x_ref, g_ref, h0_ref, buf_ref, wgate_ref, wx_ref, cw_ref, cb_ref, wa_ref, ba_ref,
                wi_ref, bi_ref, sp_ref, wout_ref, y_ref, hl_ref, xl_ref,
                xe_sc, a_sc, u_sc, hs_sc, carry_sc, *, segmented, gw):
    bt = x_ref.shape[0]
    ec = wx_ref.shape[1]
    x = x_ref[...]
    n = (_rms(x) * g_ref[...]).astype(BF)
    gate = _gelu(_dot(n, wgate_ref[...]))
    xb = _dot(n, wx_ref[...])
    cw = cw_ref[...]
    cb = cb_ref[...]
    ngr = bt // SUBLANES

    if segmented:
        xe_sc[SUBLANES:, :] = xb
        xl_ref[...] = xb

        def conv_group(gi, _):
            base = pl.multiple_of(gi * SUBLANES, SUBLANES)
            cur = xe_sc[pl.ds(SUBLANES + base, SUBLANES), :]
            hist = buf_ref[gi]
            a_sc[0:SUBLANES, :] = hist
            a_sc[SUBLANES:2 * SUBLANES, :] = cur
            acc = cb + cw[3:4, :] * cur
            for dly in (1, 2, 3):
                acc = acc + cw[3 - dly:4 - dly, :] * a_sc[pl.ds(SUBLANES - dly, SUBLANES), :]
            u_sc[pl.ds(base, SUBLANES), :] = acc
            return 0

        lax.fori_loop(0, ngr, conv_group, 0)
        xc = u_sc[...]
    else:
        t_idx = pl.program_id(1)

        @pl.when(t_idx == 0)
        def _():
            xe_sc[0:SUBLANES, :] = jnp.zeros((SUBLANES, ec), F32)
            carry_sc[...] = jnp.zeros(carry_sc.shape, F32)

        xe_sc[SUBLANES:, :] = xb
        xc = cb + cw[3:4, :] * xb
        for dly in (1, 2, 3):
            xc = xc + cw[3 - dly:4 - dly, :] * xe_sc[pl.ds(SUBLANES - dly, bt), :]
        xl_ref[...] = xb[bt - SUBLANES:, :]
        xe_sc[0:SUBLANES, :] = xb[bt - SUBLANES:, :]

    xcb = xc.astype(BF)
    ra, ri = [], []
    for k in range(ec // gw):
        blk = xcb[:, k * gw:(k + 1) * gw]
        ra.append(_dot(blk, wa_ref[k]))
        ri.append(_dot(blk, wi_ref[k]))
    r = _sigmoid(jnp.concatenate(ra, axis=1) + ba_ref[...])
    ig = _sigmoid(jnp.concatenate(ri, axis=1) + bi_ref[...])
    log_a = (-C_RG) * r * sp_ref[...]
    a = jnp.exp(log_a)
    mult = jnp.sqrt(-jnp.tanh(log_a) * (a * a + 1.0))
    a_sc[0:bt, :] = a
    u_sc[...] = mult * ig * xc

    if segmented:
        def scan_group(gi, _):
            base = pl.multiple_of(gi * SUBLANES, SUBLANES)
            A, B = _scan8(a_sc[pl.ds(base, SUBLANES), :], u_sc[pl.ds(base, SUBLANES), :])
            h8 = A * h0_ref[pl.ds(gi, 1), :] + B
            hs_sc[pl.ds(base, SUBLANES), :] = h8
            hl_ref[pl.ds(gi, 1), :] = h8[SUBLANES - 1:SUBLANES, :]
            return 0

        lax.fori_loop(0, ngr, scan_group, 0, unroll=4)
    else:
        def scan_group(gi, carry):
            base = pl.multiple_of(gi * SUBLANES, SUBLANES)
            A, B = _scan8(a_sc[pl.ds(base, SUBLANES), :], u_sc[pl.ds(base, SUBLANES), :])
            h8 = A * carry + B
            hs_sc[pl.ds(base, SUBLANES), :] = h8
            return h8[SUBLANES - 1:SUBLANES, :]

        last = lax.fori_loop(0, ngr, scan_group, carry_sc[...], unroll=4)
        carry_sc[...] = last
        hl_ref[...] = last

    y_ref[...] = x + _dot((hs_sc[...] * gate).astype(BF), wout_ref[...])


def _rglru(x, g, h0, buf8, w, *, bt, batch, seq, segmented):
    d = x.shape[1]
    ec = w['wx'].shape[1]
    gw = w['wa'].shape[1]
    names = ['wgate', 'wx', 'cw', 'cb', 'wa', 'ba', 'wi', 'bi', 'sp', 'wout']
    wspecs = [_full(w[k].shape) for k in names]
    if segmented:
        t = x.shape[0]
        nseq = t // SUBLANES
        sb = bt // SUBLANES
        grid = (t // bt,)
        in_specs = [pl.BlockSpec((bt, d), lambda i: (i, 0)), _full(g.shape),
                    pl.BlockSpec((sb, ec), lambda i: (i, 0)),
                    pl.BlockSpec((sb, SUBLANES, ec), lambda i: (i, 0, 0))] + wspecs
        out_shape = (jax.ShapeDtypeStruct((t, d), F32), jax.ShapeDtypeStruct((nseq, ec), F32),
                     jax.ShapeDtypeStruct((t, ec), F32))
        out_specs = (pl.BlockSpec((bt, d), lambda i: (i, 0)), pl.BlockSpec((sb, ec), lambda i: (i, 0)),
                     pl.BlockSpec((bt, ec), lambda i: (i, 0)))
        sem = ("parallel",)
    else:
        nt = seq // bt
        grid = (batch, nt)
        in_specs = [pl.BlockSpec((bt, d), lambda b, i: (b * nt + i, 0)), _full(g.shape), _full(h0.shape),
                    _full(buf8.shape)] + wspecs
        out_shape = (jax.ShapeDtypeStruct((batch * seq, d), F32), jax.ShapeDtypeStruct((batch, 1, ec), F32),
                     jax.ShapeDtypeStruct((batch, SUBLANES, ec), F32))
        out_specs = (pl.BlockSpec((bt, d), lambda b, i: (b * nt + i, 0)),
                     pl.BlockSpec((None, 1, ec), lambda b, i: (b, 0, 0)),
                     pl.BlockSpec((None, SUBLANES, ec), lambda b, i: (b, 0, 0)))
        sem = ("parallel", "arbitrary")
    scratch = [pltpu.VMEM((bt + SUBLANES, ec), F32), pltpu.VMEM((max(bt, 2 * SUBLANES), ec), F32),
               pltpu.VMEM((bt, ec), F32), pltpu.VMEM((bt, ec), F32), pltpu.VMEM((1, ec), F32)]
    return pl.pallas_call(
        functools.partial(_rglru_body, segmented=segmented, gw=gw),
        grid=grid, in_specs=in_specs, out_specs=out_specs, out_shape=out_shape, scratch_shapes=scratch,
        compiler_params=_cparams(sem), name="rglru_seg" if segmented else "rglru",
    )(x, g, h0, buf8, *[w[k] for k in names])


def _row(v):
    return v.reshape(1, -1).astype(F32)


def _pad_lanes(v, width=LANES):
    return jnp.pad(v, [(0, 0)] * (v.ndim - 1) + [(0, width - v.shape[-1])])


def _prep_mla(g_mix_l, wq_a, g_qa, wq_b, wkv_a, g_kva, w_uk, w_uv, wo, g_qn, g_qr, g_kn, g_kr):
    qr, h, dq = wq_b.shape
    kvr = w_uk.shape[0]
    nope = w_uk.shape[2]
    rope = dq - nope
    w = {
        'gmix': _row(g_mix_l),
        'wqa': wq_a.astype(BF), 'gqa': _row(g_qa),
        'wqn': wq_b[:, :, :nope].reshape(qr, h * nope).astype(BF),
        'wqr': _pad_lanes(wq_b[:, :, nope:]).reshape(qr, h * LANES).astype(BF),
        'wkvc': wkv_a[:, :kvr].astype(BF),
        'wkvr': _pad_lanes(wkv_a[:, kvr:]).astype(BF),
        'gkva': _row(g_kva), 'gkr': _pad_lanes(_row(g_kr)),
        'wukf': w_uk.reshape(kvr, h * nope).astype(BF),
        'wukt': jnp.transpose(w_uk, (1, 2, 0)).astype(BF),
        'gqn': _row(g_qn), 'gqr': _pad_lanes(_row(g_qr)), 'gkn': _row(g_kn),
    }
    wuv = jnp.transpose(w_uv, (1, 0, 2)).astype(BF)
    wo2 = wo.reshape(-1, wo.shape[-1]).astype(BF)
    return w, wuv, wo2, rope


def _rope_tables(pos, rope):
    half = rope // 2
    inv_freq = ROPE_THETA ** (-jnp.arange(half, dtype=F32) / half)
    ang = pos.astype(F32)[:, None] * inv_freq[None, :]
    cos, sin = jnp.cos(ang), jnp.sin(ang)
    cos_t = _pad_lanes(jnp.concatenate([cos, cos], axis=1))
    sin_t = _pad_lanes(jnp.concatenate([-sin, sin], axis=1))
    return cos_t, sin_t


def _mla_layer(x, j, cos_t, sin_t, prm, cache_kv, cache_ks, page_table, dims):
    batch, seq, bd, ns = dims
    tp = batch * seq
    w, wuv, wo2, rope = prm
    h, kvr, _ = wuv.shape
    scale = float(w['wukt'].shape[1] + rope) ** -0.5
    qlat, qpe, ckv, kpe, ks = _mla_proj(x, cos_t, sin_t, w, bm=512)
    kv_row = jnp.concatenate([ckv, kpe[:, :rope]], axis=1)
    ks8 = ks[:, :h]
    c_bf = ckv[:tp].astype(BF)
    kpe_bf = kpe[:tp].astype(BF)
    kst = jnp.transpose(ks8[:tp])
    y_p = _flash_prompt(qlat, qpe, c_bf, kpe_bf, kst, x, wuv, wo2, batch=batch, seq=seq, tq=FLASH_TQ,
                        tk=FLASH_TK, hc=FLASH_HC, scale=scale)
    page = cache_kv.shape[3]
    qlat_s =jnp.transpose(qlat[:, tp:].reshape(h, bd, ns, kvr), (1, 0, 2, 3)).reshape(bd, h * ns, kvr)
    qpe_s = jnp.transpose(qpe[:, tp:].reshape(h, bd, ns, LANES), (1, 0, 2, 3)).reshape(bd, h * ns, LANES)
    newkv = jnp.pad(jnp.transpose(kv_row[tp:].reshape(bd, ns, kvr + rope), (0, 2, 1)),
                    ((0, 0), (0, 0), (0, page - ns)))
    newkst = jnp.pad(jnp.transpose(ks8[tp:].reshape(bd, ns, h), (0, 2, 1)), ((0, 0), (0, 0), (0, page - ns)))
    o_s = _decode_attend(page_table, qlat_s, qpe_s, cache_kv, j, cache_ks, newkv, newkst, pps=DECODE_PPS,
                         scale=scale)
    o_s = jnp.transpose(o_s.reshape(bd, h, ns, kvr), (0, 2, 1, 3)).reshape(bd * ns, h * kvr)
    y_s = _mla_out(o_s, x[tp:], wuv, wo2, bm=min(512, bd * ns))
    y = jnp.concatenate([y_p, y_s], axis=0)
    return y, kv_row, ks8


def _gmlp_layer(x, g_mix_l, w_in, g_v, w_s, b_s, w_out, dims):
    batch, seq, bd, ns = dims
    tp = batch * seq
    groups, chunk, _ = w_s.shape
    tri = jnp.tril(jnp.ones((chunk, chunk), bool))
    ws_p = jnp.where(tri[None], w_s, 0).astype(BF)
    bs_p = _pad_lanes(jnp.transpose(b_s))
    l = min(ns, chunk)
    tri_s = jnp.tril(jnp.ones((l, l), bool))
    ws_small = jnp.where(tri_s[None], w_s[:, :l, :l], 0)
    eye = jnp.eye(chunk // l, dtype=F32)
    ws_s = jnp.einsum('ab,gts->gatbs', eye, ws_small).reshape(groups, chunk, chunk).astype(BF)
    bs_s = _pad_lanes(jnp.tile(jnp.transpose(b_s[:, :l]), (chunk // l, 1)))
    args = (_row(g_mix_l), w_in.astype(BF), _row(g_v))
    wout = w_out.astype(BF)
    (y_p,) = _gmlp(x[:tp], *args, ws_p, bs_p, wout, bm=256, emit_v=False)
    y_s, v_s = _gmlp(x[tp:], *args, ws_s, bs_s, wout, bm=256, emit_v=True)
    return jnp.concatenate([y_p, y_s], axis=0), v_s


def _rglru_layer(x, g_mix_l, h0_s, buf_s, w_gate, w_x, conv_w, conv_b, w_a, b_a, w_i, b_i, lam, w_out, dims):
    batch, seq, bd, ns = dims
    tp = batch * seq
    nb, db, _ = w_a.shape
    ec = nb * db
    pair = 2
    gw = db * pair

    def blockdiag(wb):
        wb = wb.reshape(nb // pair, pair, db, db)
        eye = jnp.eye(pair, dtype=wb.dtype)
        return jnp.einsum('kpde,pq->kpdqe', wb, eye).reshape(nb // pair, gw, gw).astype(BF)

    w = {'wgate': w_gate.astype(BF), 'wx': w_x.astype(BF), 'cw': _pad_rows(conv_w), 'cb': _row(conv_b),
         'wa': blockdiag(w_a), 'ba': _row(b_a), 'wi': blockdiag(w_i), 'bi': _row(b_i),
         'sp': _row(jax.nn.softplus(-lam.astype(F32))), 'wout': w_out.astype(BF)}
    g = _row(g_mix_l)
    cw = conv_w.shape[0]
    zeros_h = jnp.zeros((SUBLANES, ec), F32)
    zeros_b = jnp.zeros((1, SUBLANES, ec), F32)
    y_p, hl_p, xl_p = _rglru(x[:tp], g, zeros_h, zeros_b, w, bt=256, batch=batch, seq=seq, segmented=False)
    buf8 = jnp.pad(buf_s.astype(F32), ((0, 0), (SUBLANES - (cw - 1), 0), (0, 0)))
    y_s, hl_s, xl_s = _rglru(x[tp:], g, h0_s.astype(F32), buf8, w, bt=256, batch=bd, seq=ns, segmented=True)
    h_p = hl_p[:, 0, :]
    conv_p = xl_p[:, SUBLANES - (cw - 1):, :]
    conv_s = xl_s.reshape(bd, ns, ec)[:, ns - (cw - 1):, :]
    return jnp.concatenate([y_p, y_s], axis=0), h_p, conv_p, hl_s, conv_s


def _pad_rows(v, rows=SUBLANES):
    return jnp.pad(v.astype(F32), ((0, rows - v.shape[0]), (0, 0)))


def _moe_layer(x, g_ffn_l, router, w1, w3, w2, layer, *, bm=512, bf=1792):
    t, d = x.shape
    ne = router.shape[1]
    f, info, cnt = _router(x, _row(g_ffn_l), router, bm=512)
    counts = cnt[0, :ne].astype(jnp.int32)
    tiles_per = (counts + bm - 1) // bm
    tile_start = jnp.cumsum(tiles_per) - tiles_per
    n_tiles = (t * TOP_K) // bm + ne
    r_pad = n_tiles * bm
    tile_ids = jnp.arange(n_tiles)
    n_used = jnp.sum(tiles_per)
    tile_expert = jnp.sum((tile_ids[:, None] >= tile_start[None, :]).astype(jnp.int32), axis=1) - 1
    last_e = jnp.max(jnp.where(tiles_per > 0, jnp.arange(ne), 0))
    tile_expert = jnp.where(tile_ids < n_used, tile_expert, last_e).astype(jnp.int32)
    base = (tile_start * bm).astype(jnp.int32)
    sel = info[:, :TOP_K].astype(jnp.int32)
    dest = jnp.take(base, sel) + info[:, TOP_K:2 * TOP_K].astype(jnp.int32)
    npc = d // SC_WORDS
    off = (jnp.arange(npc, dtype=jnp.int32) * r_pad)[:, None]
    idx_a = (off + dest[None, :, 0]).reshape(-1)
    idx_b = (off + dest[None, :, 1]).reshape(-1)
    xs = _sc_dispatch(f.reshape(npc * t, SC_WORDS), idx_a, idx_b, npc * r_pad).reshape(npc, r_pad, SC_WORDS)
    ys = _ffn_grouped(tile_expert, n_used.reshape(1).astype(jnp.int32), xs,
                      w1, w3, w2, layer, bm=bm, bf=bf)
    ys_p = ys.reshape(npc * r_pad, SC_WORDS)
    ya = _sc_gather(ys_p, idx_a).reshape(npc, t, SC_WORDS)
    yb = _sc_gather(ys_p, idx_b).reshape(npc, t, SC_WORDS)
    return _moe_combine(x, ya, yb, info, bm=512)


def kernel(x_prompt, x_sample, cache_mla_kv, cache_mla_kscale, state_rglru_h, state_rglru_conv, page_table, g_mix, g_ffn, mla_wq_a, mla_g_qa, mla_wq_b, mla_wkv_a, mla_g_kva, mla_w_uk, mla_w_uv, mla_wo, mla_g_qn, mla_g_qr, mla_g_kn, mla_g_kr, gm_w_in, gm_g_v, gm_w_s, gm_b_s, gm_w_out, rg_w_gate, rg_w_x, rg_conv_w, rg_conv_b, rg_w_a, rg_b_a, rg_w_i, rg_b_i, rg_lam, rg_w_out, ffd_w1, ffd_w3, ffd_w2, moe_router, moe_w1, moe_w3, moe_w2):
    batch, seq, d = x_prompt.shape
    bd, ns, _ = x_sample.shape
    dims = (batch, seq, bd, ns)
    tp = batch * seq
    depth = g_mix.shape[0]
    n_mixers = 3
    past_len = page_table.shape[1] * cache_mla_kv.shape[2]
    cache_kv_t = jnp.swapaxes(cache_mla_kv, 2, 3)
    cache_ks_t = jnp.swapaxes(cache_mla_kscale, 2, 3)
    moe_w1_b, moe_w3_b, moe_w2_b = moe_w1.astype(BF), moe_w3.astype(BF), moe_w2.astype(BF)

    x = jnp.concatenate([x_prompt.reshape(tp, d), x_sample.reshape(bd * ns, d)], axis=0)
    rope = mla_wq_b.shape[-1] - mla_w_uk.shape[-1]
    pos = jnp.concatenate([jnp.tile(jnp.arange(seq, dtype=jnp.int32), batch),
                           jnp.tile(past_len + jnp.arange(ns, dtype=jnp.int32), bd)])
    cos_t, sin_t = _rope_tables(pos, rope)

    kv_out, ks_out, v_out, hp_out, cp_out, hs_out, cs_out = [], [], [], [], [], [], []
    counts = [0, 0, 0]
    for layer in range(depth):
        kind = layer % n_mixers
        j = counts[kind]
        counts[kind] += 1
        if kind == 0:
            prm = _prep_mla(g_mix[layer], mla_wq_a[j], mla_g_qa[j], mla_wq_b[j], mla_wkv_a[j], mla_g_kva[j],
                            mla_w_uk[j], mla_w_uv[j], mla_wo[j], mla_g_qn[j], mla_g_qr[j], mla_g_kn[j],
                            mla_g_kr[j])
            x, kv_row, ks8 = _mla_layer(x, j, cos_t, sin_t, prm, cache_kv_t, cache_ks_t, page_table, dims)
            kv_out.append(kv_row)
            ks_out.append(ks8)
        elif kind == 1:
            x, v_s = _gmlp_layer(x, g_mix[layer], gm_w_in[j], gm_g_v[j], gm_w_s[j], gm_b_s[j], gm_w_out[j], dims)
            v_out.append(v_s.reshape(bd, ns, -1))
        else:
            x, h_p, c_p, h_s, c_s = _rglru_layer(
                x, g_mix[layer], state_rglru_h[j], state_rglru_conv[j], rg_w_gate[j], rg_w_x[j], rg_conv_w[j],
                rg_conv_b[j], rg_w_a[j], rg_b_a[j], rg_w_i[j], rg_b_i[j], rg_lam[j], rg_w_out[j], dims)
            hp_out.append(h_p)
            cp_out.append(c_p)
            hs_out.append(h_s)
            cs_out.append(c_s)
        i = layer // 2
        if layer % 2 == 0:
            x = _ffn_dense(x, _row(g_ffn[layer]), ffd_w1[i].astype(BF), ffd_w3[i].astype(BF),
                           ffd_w2[i].astype(BF), bm=512, bf=1408)
        else:
            x = _moe_layer(x, g_ffn[layer], moe_router[i], moe_w1_b, moe_w3_b, moe_w2_b, i)

    kv_all = jnp.stack(kv_out)
    ks_all = jnp.stack(ks_out)
    return (x[:tp].reshape(batch, seq, d), x[tp:].reshape(bd, ns, d),
            kv_all[:, :tp].reshape(len(kv_out), batch, seq, -1), ks_all[:, :tp].reshape(len(ks_out), batch, seq, -1),
            kv_all[:, tp:].reshape(len(kv_out), bd, ns, -1), ks_all[:, tp:].reshape(len(ks_out), bd, ns, -1),
            jnp.stack(v_out), jnp.stack(hp_out), jnp.stack(cp_out), jnp.stack(hs_out), jnp.stack(cs_out))
```

```python
import functools

import jax
import jax.numpy as jnp
from jax import lax
from jax.experimental import pallas as pl
from jax.experimental.pallas import tpu as pltpu
from jax.experimental.pallas import tpu_sc as plsc

BF = jnp.bfloat16
F32 = jnp.float32
EPS = 1e-6
ROPE_THETA = 10000.0
C_RG = 8.0
TOP_K = 2
NEG_BIG = -1e30
LOG2E = 1.4426950408889634

LANES = 128
SUBLANES = 8
VMEM_LIMIT = 56 * 1024 * 1024

FLASH_TQ = 256
FLASH_TK = 2048
FLASH_HC = 1
DECODE_PPS = 64


def _cparams(sem):
    return pltpu.CompilerParams(dimension_semantics=sem, vmem_limit_bytes=VMEM_LIMIT)


def _dot(a, b):
    return jnp.dot(a, b, preferred_element_type=F32)


def _dot_nt(a, b):
    return lax.dot_general(a, b, (((1,), (1,)), ((), ())), preferred_element_type=F32)


def _rms(x):
    return x * lax.rsqrt(jnp.mean(x * x, axis=-1, keepdims=True) + EPS)


def _rms_half(x):
    return x * lax.rsqrt(jnp.sum(x * x, axis=-1, keepdims=True) * (1.0 / 64.0) + EPS)


def _rope_half(x, cos, sin):
    lane = lax.broadcasted_iota(jnp.int32, x.shape, 1)
    rot = jnp.where(lane < 32, pltpu.roll(x, 96, 1), pltpu.roll(x, 32, 1))
    return x * cos + rot * sin


def _gelu(x):
    return 0.5 * x * (1.0 + jnp.tanh(0.7978845608028654 * (x + 0.044715 * (x * x * x))))


def _sigmoid(x):
    return 1.0 / (1.0 + jnp.exp(-x))


def _full(shape):
    nd = len(shape)
    return pl.BlockSpec(shape, lambda *_: (0,) * nd)


def _mla_proj_body(x_ref, cos_ref, sin_ref, gmix_ref, wqa_ref, gqa_ref, wqn_ref, wqr_ref,
                   wkvc_ref, wkvr_ref, gkva_ref, gkr_ref, wukf_ref, wukt_ref, gqn_ref, gqr_ref,
                   gkn_ref, qlat_ref, qpe_ref, ckv_ref, kpe_ref, ks_ref, *, n_heads):
    x = x_ref[...]
    n = (_rms(x) * gmix_ref[...]).astype(BF)
    cos = cos_ref[...]
    sin = sin_ref[...]
    cq = (_rms(_dot(n, wqa_ref[...])) * gqa_ref[...]).astype(BF)
    ckv = _rms(_dot(n, wkvc_ref[...])) * gkva_ref[...]
    ckv_ref[...] = ckv
    kr = _dot(n, wkvr_ref[...])
    kpe_ref[...] = _rope_half(_rms_half(kr) * gkr_ref[...], cos, sin)
    kn = _dot(ckv.astype(BF), wukf_ref[...])
    lane = lax.broadcasted_iota(jnp.int32, (x.shape[0], LANES), 1)
    ks = jnp.zeros((x.shape[0], LANES), F32)
    for h in range(n_heads):
        blk = kn[:, h * 128:(h + 1) * 128]
        ksh = lax.rsqrt(jnp.mean(blk * blk, axis=-1, keepdims=True) + EPS)
        ks = jnp.where(lane == h, ksh, ks)
    ks_ref[...] = ks
    qn = _dot(cq, wqn_ref[...])
    qr = _dot(cq, wqr_ref[...])
    gq = gqn_ref[...]
    gk = gkn_ref[...]
    for h in range(n_heads):
        a = (_rms(qn[:, h * 128:(h + 1) * 128]) * gq) * gk
        qlat_ref[h] = _dot(a.astype(BF), wukt_ref[h]).astype(BF)
        r = _rms_half(qr[:, h * 128:(h + 1) * 128]) * gqr_ref[...]
        qpe_ref[h] = _rope_half(r, cos, sin).astype(BF)


def _mla_proj(x, cos, sin, w, *, bm):
    t, d = x.shape
    h = w['wukt'].shape[0]
    kvr = w['wkvc'].shape[1]
    row = lambda i: (i, 0)
    in_specs = [pl.BlockSpec((bm, d), row), pl.BlockSpec((bm, LANES), row), pl.BlockSpec((bm, LANES), row)]
    names = ['gmix', 'wqa', 'gqa', 'wqn', 'wqr', 'wkvc', 'wkvr', 'gkva', 'gkr', 'wukf', 'wukt', 'gqn', 'gqr', 'gkn']
    in_specs += [_full(w[k].shape) for k in names]
    out_shape = (jax.ShapeDtypeStruct((h, t, kvr), BF), jax.ShapeDtypeStruct((h, t, LANES), BF),
                 jax.ShapeDtypeStruct((t, kvr), F32), jax.ShapeDtypeStruct((t, LANES), F32),
                 jax.ShapeDtypeStruct((t, LANES), F32))
    out_specs = (pl.BlockSpec((h, bm, kvr), lambda i: (0, i, 0)), pl.BlockSpec((h, bm, LANES), lambda i: (0, i, 0)),
                 pl.BlockSpec((bm, kvr), row), pl.BlockSpec((bm, LANES), row), pl.BlockSpec((bm, LANES), row))
    return pl.pallas_call(
        functools.partial(_mla_proj_body, n_heads=h),
        grid=(t // bm,), in_specs=in_specs, out_specs=out_specs, out_shape=out_shape,
        compiler_params=_cparams(("parallel",)), name="mla_proj",
    )(x, cos, sin, *[w[k] for k in names])


def _mla_out_rows(o_heads, wuv_ref, wo_ref, obuf_ref):
    for h, oh in enumerate(o_heads):
        v = _dot(oh.astype(BF), wuv_ref[h])
        obuf_ref[:, h * 128:(h + 1) * 128] = v.astype(BF)
    return _dot(obuf_ref[...], wo_ref[...])


def _mla_out_body(o_ref, x_ref, wuv_ref, wo_ref, y_ref, obuf_ref, *, n_heads, kvr):
    o = o_ref[...]
    heads = [o[:, h * kvr:(h + 1) * kvr] for h in range(n_heads)]
    y_ref[...] = x_ref[...] + _mla_out_rows(heads, wuv_ref, wo_ref, obuf_ref)


def _mla_out(o, x, wuv, wo, *, bm):
    t, d = x.shape
    h, kvr, vd = wuv.shape
    return pl.pallas_call(
        functools.partial(_mla_out_body, n_heads=h, kvr=kvr),
        grid=(t // bm,),
        in_specs=[pl.BlockSpec((bm, h * kvr), lambda i: (i, 0)), pl.BlockSpec((bm, d), lambda i: (i, 0)),
                  _full(wuv.shape), _full(wo.shape)],
        out_specs=pl.BlockSpec((bm, d), lambda i: (i, 0)),
        out_shape=jax.ShapeDtypeStruct((t, d), F32),
        scratch_shapes=[pltpu.VMEM((bm, h * vd), BF)],
        compiler_params=_cparams(("parallel",)), name="mla_out",
    )(o, x, wuv, wo)


def _flash_body(qi_ref, kj_ref, qlat_ref, qpe_ref, c_ref, kpe_ref, kst_ref, x_ref, wuv_ref, wo_ref, y_ref,
                m_sc, l_sc, acc_sc, obuf_ref, *, tq, tk, n_heads, hc, c2):
    pidx = pl.program_id(1)
    i = qi_ref[pidx]
    j = kj_ref[pidx]
    last_needed = (i * tq + tq - 1) // tk
    kvr = qlat_ref.shape[-1]

    @pl.when(j == 0)
    def _():
        m_sc[...] = jnp.full(m_sc.shape, NEG_BIG, F32)
        l_sc[...] = jnp.zeros(l_sc.shape, F32)
        acc_sc[...] = jnp.zeros(acc_sc.shape, F32)

    def step(masked):
        c = c_ref[...]
        kpe = kpe_ref[...]
        kst = kst_ref[...] * c2
        if masked:
            qpos = i * tq + lax.broadcasted_iota(jnp.int32, (tq, tk), 0)
            kpos = j * tk + lax.broadcasted_iota(jnp.int32, (tq, tk), 1)
            keep = kpos <= qpos
        for g in range(n_heads // hc):
            r0, nr = g * hc * tq, hc * tq
            q = qlat_ref[g * hc:(g + 1) * hc].reshape(nr, kvr)
            qp = qpe_ref[g * hc:(g + 1) * hc].reshape(nr, LANES)
            s_n = _dot_nt(q, c)
            s_p = _dot_nt(qp, kpe)
            p_rows = []
            for hh in range(hc):
                h = g * hc + hh
                hr = slice(hh * tq, (hh + 1) * tq)
                pieces = []
                mx = None
                for kt in range(tk // LANES):
                    cols = slice(kt * LANES, (kt + 1) * LANES)
                    sh = s_n[hr, cols] * kst[h:h + 1, cols] + s_p[hr, cols] * c2
                    if masked:
                        sh = jnp.where(keep[:, cols], sh, NEG_BIG)
                    pieces.append(sh)
                    mx = sh if mx is None else jnp.maximum(mx, sh)
                hs = slice(h * tq, (h + 1) * tq)
                m_prev = m_sc[hs]
                m_cur = jnp.maximum(m_prev, jnp.max(mx, axis=1, keepdims=True))
                alpha = jnp.exp2(m_prev - m_cur)
                ps = []
                sm = None
                for sh in pieces:
                    p = jnp.exp2(sh - m_cur)
                    sm = p if sm is None else sm + p
                    ps.append(p.astype(BF))
                l_sc[hs] = alpha * l_sc[hs] + jnp.sum(sm, axis=1, keepdims=True)
                m_sc[hs] = m_cur
                acc_sc[hs] = alpha * acc_sc[hs]
                p_rows.append(jnp.concatenate(ps, axis=1))
            p_all = jnp.concatenate(p_rows, axis=0) if hc > 1 else p_rows[0]
            acc_sc[r0:r0 + nr] += _dot(p_all, c)

    @pl.when(j < last_needed)
    def _():
        step(False)

    @pl.when(j == last_needed)
    def _():
        step(True)
        inv = 1.0 / l_sc[...]
        heads = [acc_sc[h * tq:(h + 1) * tq, :] * inv[h * tq:(h + 1) * tq] for h in range(n_heads)]
        y_ref[...] = x_ref[...] + _mla_out_rows(heads, wuv_ref, wo_ref, obuf_ref)


def _flash_prompt(qlat, qpe, c_bf, kpe_bf, kst, x, wuv, wo, *, batch, seq, tq, tk, hc, scale):
    h, _, kvr = qlat.shape
    d = x.shape[1]
    nq = seq // tq
    nk = seq // tk
    pairs = [(i, j) for i in range(nq) for j in range((i * tq + tq - 1) // tk + 1)]
    qi = jnp.asarray([p[0] for p in pairs], jnp.int32)
    kj = jnp.asarray([p[1] for p in pairs], jnp.int32)
    in_specs = [
        pl.BlockSpec((h, tq, kvr), lambda b, p, qi, kj: (0, b * nq + qi[p], 0)),
        pl.BlockSpec((h, tq, LANES), lambda b, p, qi, kj: (0, b * nq + qi[p], 0)),
        pl.BlockSpec((tk, kvr), lambda b, p, qi, kj: (b * nk + kj[p], 0)),
        pl.BlockSpec((tk, LANES), lambda b, p, qi, kj: (b * nk + kj[p], 0)),
        pl.BlockSpec((h, tk), lambda b, p, qi, kj: (0, b * nk + kj[p])),
        pl.BlockSpec((tq, d), lambda b, p, qi, kj: (b * nq + qi[p], 0)),
        pl.BlockSpec(wuv.shape, lambda b, p, qi, kj: (0, 0, 0)),
        pl.BlockSpec(wo.shape, lambda b, p, qi, kj: (0, 0)),
    ]
    grid_spec = pltpu.PrefetchScalarGridSpec(
        num_scalar_prefetch=2, grid=(batch, len(pairs)), in_specs=in_specs,
        out_specs=pl.BlockSpec((tq, d), lambda b, p, qi, kj: (b * nq + qi[p], 0)),
        scratch_shapes=[pltpu.VMEM((h * tq, 1), F32), pltpu.VMEM((h * tq, 1), F32),
                        pltpu.VMEM((h * tq, kvr), F32), pltpu.VMEM((tq, wo.shape[0]), BF)])
    return pl.pallas_call(
        functools.partial(_flash_body, tq=tq, tk=tk, n_heads=h, hc=hc, c2=scale * LOG2E),
        grid_spec=grid_spec, out_shape=jax.ShapeDtypeStruct((batch * seq, d), F32),
        compiler_params=_cparams(("parallel", "arbitrary")), name="mla_flash_prompt",
    )(qi, kj, qlat, qpe, c_bf, kpe_bf, kst, x, wuv, wo)


def _decode_body(pt_ref, q_ref, qp_ref, *rest, pps, page, n_heads, n_new, kvr, rope, c2):
    kv_refs = rest[:pps]
    ks_refs = rest[pps:2 * pps]
    newkv_ref, newks_ref, o_ref, kbuf, kst_sc, m_sc, l_sc, acc_sc = rest[2 * pps:]
    cidx = pl.program_id(1)
    nc = pl.num_programs(1)
    rows = n_heads * n_new

    @pl.when(cidx == 0)
    def _():
        m_sc[...] = jnp.full(m_sc.shape, NEG_BIG, F32)
        l_sc[...] = jnp.zeros(l_sc.shape, F32)
        acc_sc[...] = jnp.zeros(acc_sc.shape, F32)

    q = q_ref[...]
    qp = qp_ref[:, :rope]

    def attend(k_all, kst, keep):
        ct = k_all[:kvr]
        s_n = _dot(q, ct)
        s_p = _dot(qp, k_all[kvr:kvr + rope])
        kst = kst * c2
        parts = []
        for h in range(n_heads):
            sl = slice(h * n_new, (h + 1) * n_new)
            parts.append(s_n[sl] * kst[h:h + 1, :] + s_p[sl] * c2)
        s = jnp.concatenate(parts, axis=0)
        if keep is not None:
            s = jnp.where(keep, s, NEG_BIG)
        m_prev = m_sc[...]
        m_cur = jnp.maximum(m_prev, jnp.max(s, axis=1, keepdims=True))
        alpha = jnp.exp2(m_prev - m_cur)
        p = jnp.exp2(s - m_cur)
        l_sc[...] = alpha * l_sc[...] + jnp.sum(p, axis=1, keepdims=True)
        acc_sc[...] = alpha * acc_sc[...] + _dot_nt(p.astype(BF), ct)
        m_sc[...] = m_cur

    for p in range(pps):
        kbuf[:, p * page:(p + 1) * page] = kv_refs[p][...].astype(BF)
        kst_sc[:, p * page:(p + 1) * page] = ks_refs[p][...]
    attend(kbuf[...], kst_sc[...], None)

    @pl.when(cidx == nc - 1)
    def _():
        r = lax.broadcasted_iota(jnp.int32, (rows, page), 0)
        kj = lax.broadcasted_iota(jnp.int32, (rows, page), 1)
        keep = kj <= (r % n_new)
        attend(newkv_ref[...].astype(BF), newks_ref[...], keep)
        o_ref[...] = acc_sc[...] * (1.0 / l_sc[...])


def _decode_attend(page_table, qlat_s, qpe_s, cache_kv, j, cache_ks, newkv, newkst, *, pps, scale):
    bd, n_pages = page_table.shape
    rows, kvr = qlat_s.shape[1:]
    width, page = cache_kv.shape[2:]
    n_heads = cache_ks.shape[2]
    n_new = rows // n_heads
    rope = width - kvr
    nc = n_pages // pps
    pt_flat = page_table.reshape(-1)

    def page_map_kv(p):
        return lambda b, c, pt: (j, pt[b * n_pages + c * pps + p], 0, 0)

    in_specs = [pl.BlockSpec((None, rows, kvr), lambda b, c, pt: (b, 0, 0)),
                pl.BlockSpec((None, rows, LANES), lambda b, c, pt: (b, 0, 0))]
    in_specs += [pl.BlockSpec((None, None, width, page), page_map_kv(p)) for p in range(pps)]
    in_specs += [pl.BlockSpec((None, None, n_heads, page), page_map_kv(p)) for p in range(pps)]
    in_specs += [pl.BlockSpec((None, width, page), lambda b, c, pt: (b, 0, 0)),
                 pl.BlockSpec((None, n_heads, page), lambda b, c, pt: (b, 0, 0))]
    grid_spec = pltpu.PrefetchScalarGridSpec(
        num_scalar_prefetch=1, grid=(bd, nc), in_specs=in_specs,
        out_specs=pl.BlockSpec((None, rows, kvr), lambda b, c, pt: (b, 0, 0)),
        scratch_shapes=[pltpu.VMEM((width, pps * page), BF), pltpu.VMEM((n_heads, pps * page), F32),
                        pltpu.VMEM((rows, 1), F32), pltpu.VMEM((rows, 1), F32), pltpu.VMEM((rows, kvr), F32)])
    return pl.pallas_call(
        functools.partial(_decode_body, pps=pps, page=page, n_heads=n_heads, n_new=n_new, kvr=kvr,
                          rope=rope, c2=scale * LOG2E),
        grid_spec=grid_spec, out_shape=jax.ShapeDtypeStruct((bd, rows, kvr), F32),
        compiler_params=_cparams(("parallel", "arbitrary")), name="mla_decode",
    )(pt_flat, qlat_s, qpe_s, *([cache_kv] * pps), *([cache_ks] * pps), newkv, newkst)


def _ffn_dense_body(x_ref, g_ref, w1_ref, w3_ref, w2_ref, o_ref, h_sc):
    j = pl.program_id(1)

    @pl.when(j == 0)
    def _():
        x = x_ref[...]
        h_sc[...] = (_rms(x) * g_ref[...]).astype(BF)
        o_ref[...] = x

    h = h_sc[...]
    a = _dot(h, w1_ref[...])
    b = _dot(h, w3_ref[...])
    g = (a * _sigmoid(a)) * b
    o_ref[...] += _dot(g.astype(BF), w2_ref[...])


def _ffn_dense(x, g, w1, w3, w2, *, bm, bf):
    t, d = x.shape
    ff = w1.shape[1]
    return pl.pallas_call(
        _ffn_dense_body, grid=(t // bm, ff // bf),
        in_specs=[pl.BlockSpec((bm, d), lambda i, j: (i, 0)), _full(g.shape),
                  pl.BlockSpec((d, bf), lambda i, j: (0, j)), pl.BlockSpec((d, bf), lambda i, j: (0, j)),
                  pl.BlockSpec((bf, d), lambda i, j: (j, 0))],
        out_specs=pl.BlockSpec((bm, d), lambda i, j: (i, 0)),
        out_shape=jax.ShapeDtypeStruct((t, d), F32),
        scratch_shapes=[pltpu.VMEM((bm, d), BF)],
        compiler_params=_cparams(("parallel", "arbitrary")), name="ffn_dense",
    )(x, g, w1, w3, w2)


def _ffn_group_body(te_ref, nt_ref, x_ref, w1_ref, w3_ref, w2_ref, o_ref, h_sc):
    i = pl.program_id(0)
    j = pl.program_id(1)

    @pl.when(i < nt_ref[0])
    def _():
        npc, _, w = x_ref.shape

        @pl.when(j == 0)
        def _():
            for c in range(npc):
                h_sc[:, c * w:(c + 1) * w] = x_ref[c].astype(BF)
            o_ref[...] = jnp.zeros(o_ref.shape, F32)

        h = h_sc[...]
        a = _dot(h, w1_ref[...])
        b = _dot(h, w3_ref[...])
        g = (a * _sigmoid(a)) * b
        y = _dot(g.astype(BF), w2_ref[...])
        for c in range(npc):
            o_ref[c] += y[:, c * w:(c + 1) * w]


def _ffn_grouped(tile_expert, n_tiles_used, xs, w1, w3, w2, layer, *, bm, bf):
    npc, r, w = xs.shape
    d = npc * w
    ff = w1.shape[3]
    grid_spec = pltpu.PrefetchScalarGridSpec(
        num_scalar_prefetch=2, grid=(r // bm, ff // bf),
        in_specs=[pl.BlockSpec((npc, bm, w), lambda i, j, te, nt: (0, i, 0)),
                  pl.BlockSpec((None, None, d, bf), lambda i, j, te, nt: (layer, te[i], 0, j)),
                  pl.BlockSpec((None, None, d, bf), lambda i, j, te, nt: (layer, te[i], 0, j)),
                  pl.BlockSpec((None, None, bf, d), lambda i, j, te, nt: (layer, te[i], j, 0))],
        out_specs=pl.BlockSpec((npc, bm, w), lambda i, j, te, nt: (0, i, 0)),
        scratch_shapes=[pltpu.VMEM((bm, d), BF)])
    return pl.pallas_call(
        _ffn_group_body, grid_spec=grid_spec, out_shape=jax.ShapeDtypeStruct((npc, r, w), F32),
        compiler_params=_cparams(("parallel", "arbitrary")), name="ffn_grouped",
    )(tile_expert, n_tiles_used, xs, w1, w3, w2)


def _router_body(x_ref, g_ref, wr_ref, tri_ref, f_ref, info_ref, cnt_ref, carry_sc, *, n_experts):
    @pl.when(pl.program_id(0) == 0)
    def _():
        carry_sc[...] = jnp.zeros(carry_sc.shape, F32)

    f = _rms(x_ref[...]) * g_ref[...]
    for c in range(f_ref.shape[0]):
        f_ref[c] = f[:, c * SC_WORDS:(c + 1) * SC_WORDS]
    lane = lax.broadcasted_iota(jnp.int32, (f.shape[0], LANES), 1)
    f_hi = f.astype(BF)
    f_lo = (f - f_hi.astype(F32)).astype(BF)
    raw = _dot(f_hi, wr_ref[0]) + (_dot(f_lo, wr_ref[0]) + _dot(f_hi, wr_ref[1]))
    logits = jnp.where(lane < n_experts, raw, -jnp.inf)
    v1 = jnp.max(logits, axis=1, keepdims=True)
    i1 = jnp.min(jnp.where(logits == v1, lane, LANES), axis=1, keepdims=True)
    rest = jnp.where(lane == i1, -jnp.inf, logits)
    v2 = jnp.max(rest, axis=1, keepdims=True)
    i2 = jnp.min(jnp.where(rest == v2, lane, LANES), axis=1, keepdims=True)
    e2 = jnp.exp(v2 - v1)
    den = 1.0 + e2
    g1 = 1.0 / den
    g2 = e2 / den
    hot1 = lane == i1
    hot2 = lane == i2
    onehot = jnp.where(hot1 | hot2, 1.0, 0.0)
    rank = _dot(tri_ref[...], onehot.astype(BF)) + carry_sc[...]
    carry = carry_sc[...] + jnp.sum(onehot, axis=0, keepdims=True)
    carry_sc[...] = carry
    cnt_ref[...] = jnp.broadcast_to(carry, cnt_ref.shape)
    r1 = jnp.sum(jnp.where(hot1, rank, 0.0), axis=1, keepdims=True)
    r2 = jnp.sum(jnp.where(hot2, rank, 0.0), axis=1, keepdims=True)
    info = jnp.zeros((f.shape[0], LANES), F32)
    for k, col in enumerate((i1.astype(F32), i2.astype(F32), r1, r2, g1, g2)):
        info = jnp.where(lane == k, col, info)
    info_ref[...] = info


def _router(x, g, wr, *, bm):
    t, d = x.shape
    n_experts = wr.shape[1]
    wr_f = _pad_lanes(wr.astype(F32))
    wr_hi = wr_f.astype(BF)
    wr_p = jnp.stack([wr_hi, (wr_f - wr_hi.astype(F32)).astype(BF)])
    tri =(jnp.arange(bm)[:, None] > jnp.arange(bm)[None, :]).astype(BF)
    return pl.pallas_call(
        functools.partial(_router_body, n_experts=n_experts), grid=(t // bm,),
        in_specs=[pl.BlockSpec((bm, d), lambda i: (i, 0)), _full(g.shape), _full(wr_p.shape), _full(tri.shape)],
        out_specs=(pl.BlockSpec((d // SC_WORDS, bm, SC_WORDS), lambda i: (0, i, 0)),
                   pl.BlockSpec((bm, LANES), lambda i: (i, 0)),
                   pl.BlockSpec((SUBLANES, LANES), lambda i: (0, 0))),
        out_shape=(jax.ShapeDtypeStruct((d // SC_WORDS, t, SC_WORDS), F32), jax.ShapeDtypeStruct((t, LANES), F32),
                   jax.ShapeDtypeStruct((SUBLANES, LANES), F32)),
        scratch_shapes=[pltpu.VMEM((1, LANES), F32)],
        compiler_params=_cparams(("arbitrary",)), name="moe_router",
    )(x, g, wr_p, tri)


SC_WORDS = 256
SC_WINDOW = 128


def _sc_mesh():
    return plsc.VectorSubcoreMesh(core_axis_name="c", subcore_axis_name="s")


def _sc_dispatch(pieces, idx_a, idx_b, n_out):
    n, w = pieces.shape

    @pl.kernel(out_type=jax.ShapeDtypeStruct((n_out, w), pieces.dtype), mesh=_sc_mesh())
    def k(x_hbm, ia_hbm, ib_hbm, o_hbm):
        def body(x_vmem, ia_vmem, ib_vmem):
            pltpu.sync_copy(x_vmem, o_hbm.at[ia_vmem.at[0]])
            pltpu.sync_copy(x_vmem, o_hbm.at[ib_vmem.at[0]])

        pltpu.emit_pipeline(
            body, grid=(n // SC_WINDOW,),
            in_specs=[pl.BlockSpec((SC_WINDOW, w), lambda i: (i, 0)),
                      pl.BlockSpec((1, SC_WINDOW), lambda i: (0, i)),
                      pl.BlockSpec((1, SC_WINDOW), lambda i: (0, i))],
            out_specs=[], core_axis_name=('c', 's'), dimension_semantics=(pltpu.PARALLEL,),
        )(x_hbm, ia_hbm, ib_hbm)

    return k(pieces, idx_a.reshape(1, n), idx_b.reshape(1, n))


def _sc_gather(table, idx):
    n = idx.shape[0]
    w = table.shape[1]

    @pl.kernel(out_type=jax.ShapeDtypeStruct((n, w), table.dtype), mesh=_sc_mesh())
    def k(t_hbm, i_hbm, o_hbm):
        def body(i_vmem, o_vmem):
            pltpu.sync_copy(t_hbm.at[i_vmem.at[0]], o_vmem)

        pltpu.emit_pipeline(
            body, grid=(n // SC_WINDOW,),
            in_specs=[pl.BlockSpec((1, SC_WINDOW), lambda i: (0, i))],
            out_specs=[pl.BlockSpec((SC_WINDOW, w), lambda i: (i, 0))],
            core_axis_name=('c', 's'), dimension_semantics=(pltpu.PARALLEL,),
        )(i_hbm, o_hbm)

    return k(table, idx.reshape(1, n))


def _combine_body(x_ref, ya_ref, yb_ref, info_ref, o_ref):
    info = info_ref[...]
    ga, gb = info[:, 4:5], info[:, 5:6]
    npc, _, w = ya_ref.shape
    for c in range(npc):
        cols = slice(c * w, (c + 1) * w)
        o_ref[:, cols] = x_ref[:, cols] + (ga * ya_ref[c] + gb * yb_ref[c])


def _moe_combine(x, ya, yb, info, *, bm):
    t, d = x.shape
    npc, _, w = ya.shape
    row = lambda i: (i, 0)
    blk = pl.BlockSpec((npc, bm, w), lambda i: (0, i, 0))
    return pl.pallas_call(
        _combine_body, grid=(t // bm,),
        in_specs=[pl.BlockSpec((bm, d), row), blk, blk, pl.BlockSpec((bm, LANES), row)],
        out_specs=pl.BlockSpec((bm, d), row), out_shape=jax.ShapeDtypeStruct((t, d), F32),
        compiler_params=_cparams(("parallel",)), name="moe_combine",
    )(x, ya, yb, info)


def _gmlp_body(x_ref, g_ref, win_ref, gv_ref, ws_ref, bs_ref, wout_ref, y_ref, *maybe_v, chunk, groups):
    x = x_ref[...]
    bm = x.shape[0]
    n = (_rms(x) * g_ref[...]).astype(BF)
    z = _gelu(_dot(n, win_ref[...]))
    e = z.shape[1] // 2
    dg = e // groups
    u = z[:, :e]
    v = _rms(z[:, e:]) * gv_ref[...]
    if maybe_v:
        maybe_v[0][...] = v
    vb = v.astype(BF)
    bs = bs_ref[...]
    rows = []
    for c in range(bm // chunk):
        cols = []
        for g in range(groups):
            mixed = _dot(ws_ref[g], vb[c * chunk:(c + 1) * chunk, g * dg:(g + 1) * dg]) + bs[:, g:g + 1]
            cols.append(u[c * chunk:(c + 1) * chunk, g * dg:(g + 1) * dg] * mixed)
        rows.append(jnp.concatenate(cols, axis=1))
    y = jnp.concatenate(rows, axis=0) if len(rows) > 1 else rows[0]
    y_ref[...] = x + _dot(y.astype(BF), wout_ref[...])


def _gmlp(x, g, win, gv, ws, bs, wout, *, bm, emit_v):
    t, d = x.shape
    e = wout.shape[0]
    groups, chunk, _ = ws.shape
    out_shape = [jax.ShapeDtypeStruct((t, d), F32)]
    out_specs = [pl.BlockSpec((bm, d), lambda i: (i, 0))]
    if emit_v:
        out_shape.append(jax.ShapeDtypeStruct((t, e), F32))
        out_specs.append(pl.BlockSpec((bm, e), lambda i: (i, 0)))
    res = pl.pallas_call(
        functools.partial(_gmlp_body, chunk=chunk, groups=groups),
        grid=(t // bm,),
        in_specs=[pl.BlockSpec((bm, d), lambda i: (i, 0)), _full(g.shape), _full(win.shape), _full(gv.shape),
                  _full(ws.shape), _full(bs.shape), _full(wout.shape)],
        out_specs=tuple(out_specs), out_shape=tuple(out_shape),
        compiler_params=_cparams(("parallel",)), name="gmlp_v" if emit_v else "gmlp",
    )(x, g, win, gv, ws, bs, wout)
    return res


def _scan8(a8, u8):
    row = lax.broadcasted_iota(jnp.int32, a8.shape, 0)
    A, B = a8, u8
    for d in (1, 2, 4):
        a_sh = pltpu.roll(A, d, 0)
        b_sh = pltpu.roll(B, d, 0)
        m = row >= d
        B = jnp.where(m, A * b_sh + B, B)
        A = jnp.where(m, A * a_sh, A)
    return A, B


def _rglru_body(x_ref, g_ref, h0_ref, buf_ref, wgate_ref, wx_ref, cw_ref, cb_ref, wa_ref, ba_ref,
                wi_ref, bi_ref, sp_ref, wout_ref, y_ref, hl_ref, xl_ref,
                xe_sc, a_sc, u_sc, hs_sc, carry_sc, *, segmented, gw):
    bt = x_ref.shape[0]
    ec = wx_ref.shape[1]
    x = x_ref[...]
    n = (_rms(x) * g_ref[...]).astype(BF)
    gate = _gelu(_dot(n, wgate_ref[...]))
    xb = _dot(n, wx_ref[...])
    cw = cw_ref[...]
    cb = cb_ref[...]
    ngr = bt // SUBLANES

    if segmented:
        xe_sc[SUBLANES:, :] = xb
        xl_ref[...] = xb

        def conv_group(gi, _):
            base = pl.multiple_of(gi * SUBLANES, SUBLANES)
            cur = xe_sc[pl.ds(SUBLANES + base, SUBLANES), :]
            hist = buf_ref[gi]
            a_sc[0:SUBLANES, :] = hist
            a_sc[SUBLANES:2 * SUBLANES, :] = cur
            acc = cb + cw[3:4, :] * cur
            for dly in (1, 2, 3):
                acc = acc + cw[3 - dly:4 - dly, :] * a_sc[pl.ds(SUBLANES - dly, SUBLANES), :]
            u_sc[pl.ds(base, SUBLANES), :] = acc
            return 0

        lax.fori_loop(0, ngr, conv_group, 0)
        xc = u_sc[...]
    else:
        t_idx = pl.program_id(1)

        @pl.when(t_idx == 0)
        def _():
            xe_sc[0:SUBLANES, :] = jnp.zeros((SUBLANES, ec), F32)
            carry_sc[...] = jnp.zeros(carry_sc.shape, F32)

        xe_sc[SUBLANES:, :] = xb
        xc = cb + cw[3:4, :] * xb
        for dly in (1, 2, 3):
            xc = xc + cw[3 - dly:4 - dly, :] * xe_sc[pl.ds(SUBLANES - dly, bt), :]
        xl_ref[...] = xb[bt - SUBLANES:, :]
        xe_sc[0:SUBLANES, :] = xb[bt - SUBLANES:, :]

    xcb = xc.astype(BF)
    ra, ri = [], []
    for k in range(ec // gw):
        blk = xcb[:, k * gw:(k + 1) * gw]
        ra.append(_dot(blk, wa_ref[k]))
        ri.append(_dot(blk, wi_ref[k]))
    r = _sigmoid(jnp.concatenate(ra, axis=1) + ba_ref[...])
    ig = _sigmoid(jnp.concatenate(ri, axis=1) + bi_ref[...])
    log_a = (-C_RG) * r * sp_ref[...]
    a = jnp.exp(log_a)
    mult = jnp.sqrt(-jnp.tanh(log_a) * (a * a + 1.0))
    a_sc[0:bt, :] = a
    u_sc[...] = mult * ig * xc

    if segmented:
        def scan_group(gi, _):
            base = pl.multiple_of(gi * SUBLANES, SUBLANES)
            A, B = _scan8(a_sc[pl.ds(base, SUBLANES), :], u_sc[pl.ds(base, SUBLANES), :])
            h8 = A * h0_ref[pl.ds(gi, 1), :] + B
            hs_sc[pl.ds(base, SUBLANES), :] = h8
            hl_ref[pl.ds(gi, 1), :] = h8[SUBLANES - 1:SUBLANES, :]
            return 0

        lax.fori_loop(0, ngr, scan_group, 0, unroll=4)
    else:
        def scan_group(gi, carry):
            base = pl.multiple_of(gi * SUBLANES, SUBLANES)
            A, B = _scan8(a_sc[pl.ds(base, SUBLANES), :], u_sc[pl.ds(base, SUBLANES), :])
            h8 = A * carry + B
            hs_sc[pl.ds(base, SUBLANES), :] = h8
            return h8[SUBLANES - 1:SUBLANES, :]

        last = lax.fori_loop(0, ngr, scan_group, carry_sc[...], unroll=4)
        carry_sc[...] = last
        hl_ref[...] = last

    y_ref[...] = x + _dot((hs_sc[...] * gate).astype(BF), wout_ref[...])


def _rglru(x, g, h0, buf8, w, *, bt, batch, seq, segmented):
    d = x.shape[1]
    ec = w['wx'].shape[1]
    gw = w['wa'].shape[1]
    names = ['wgate', 'wx', 'cw', 'cb', 'wa', 'ba', 'wi', 'bi', 'sp', 'wout']
    wspecs = [_full(w[k].shape) for k in names]
    if segmented:
        t = x.shape[0]
        nseq = t // SUBLANES
        sb = bt // SUBLANES
        grid = (t // bt,)
        in_specs = [pl.BlockSpec((bt, d), lambda i: (i, 0)), _full(g.shape),
                    pl.BlockSpec((sb, ec), lambda i: (i, 0)),
                    pl.BlockSpec((sb, SUBLANES, ec), lambda i: (i, 0, 0))] + wspecs
        out_shape = (jax.ShapeDtypeStruct((t, d), F32), jax.ShapeDtypeStruct((nseq, ec), F32),
                     jax.ShapeDtypeStruct((t, ec), F32))
        out_specs = (pl.BlockSpec((bt, d), lambda i: (i, 0)), pl.BlockSpec((sb, ec), lambda i: (i, 0)),
                     pl.BlockSpec((bt, ec), lambda i: (i, 0)))
        sem = ("parallel",)
    else:
        nt = seq // bt
        grid = (batch, nt)
        in_specs = [pl.BlockSpec((bt, d), lambda b, i: (b * nt + i, 0)), _full(g.shape), _full(h0.shape),
                    _full(buf8.shape)] + wspecs
        out_shape = (jax.ShapeDtypeStruct((batch * seq, d), F32), jax.ShapeDtypeStruct((batch, 1, ec), F32),
                     jax.ShapeDtypeStruct((batch, SUBLANES, ec), F32))
        out_specs = (pl.BlockSpec((bt, d), lambda b, i: (b * nt + i, 0)),
                     pl.BlockSpec((None, 1, ec), lambda b, i: (b, 0, 0)),
                     pl.BlockSpec((None, SUBLANES, ec), lambda b, i: (b, 0, 0)))
        sem = ("parallel", "arbitrary")
    scratch = [pltpu.VMEM((bt + SUBLANES, ec), F32), pltpu.VMEM((max(bt, 2 * SUBLANES), ec), F32),
               pltpu.VMEM((bt, ec), F32), pltpu.VMEM((bt, ec), F32), pltpu.VMEM((1, ec), F32)]
    return pl.pallas_call(
        functools.partial(_rglru_body, segmented=segmented, gw=gw),
        grid=grid, in_specs=in_specs, out_specs=out_specs, out_shape=out_shape, scratch_shapes=scratch,
        compiler_params=_cparams(sem), name="rglru_seg" if segmented else "rglru",
    )(x, g, h0, buf8, *[w[k] for k in names])


def _row(v):
    return v.reshape(1, -1).astype(F32)


def _pad_lanes(v, width=LANES):
    return jnp.pad(v, [(0, 0)] * (v.ndim - 1) + [(0, width - v.shape[-1])])


def _prep_mla(g_mix_l, wq_a, g_qa, wq_b, wkv_a, g_kva, w_uk, w_uv, wo, g_qn, g_qr, g_kn, g_kr):
    qr, h, dq = wq_b.shape
    kvr = w_uk.shape[0]
    nope = w_uk.shape[2]
    rope = dq - nope
    w = {
        'gmix': _row(g_mix_l),
        'wqa': wq_a.astype(BF), 'gqa': _row(g_qa),
        'wqn': wq_b[:, :, :nope].reshape(qr, h * nope).astype(BF),
        'wqr': _pad_lanes(wq_b[:, :, nope:]).reshape(qr, h * LANES).astype(BF),
        'wkvc': wkv_a[:, :kvr].astype(BF),
        'wkvr': _pad_lanes(wkv_a[:, kvr:]).astype(BF),
        'gkva': _row(g_kva), 'gkr': _pad_lanes(_row(g_kr)),
        'wukf': w_uk.reshape(kvr, h * nope).astype(BF),
        'wukt': jnp.transpose(w_uk, (1, 2, 0)).astype(BF),
        'gqn': _row(g_qn), 'gqr': _pad_lanes(_row(g_qr)), 'gkn': _row(g_kn),
    }
    wuv = jnp.transpose(w_uv, (1, 0, 2)).astype(BF)
    wo2 = wo.reshape(-1, wo.shape[-1]).astype(BF)
    return w, wuv, wo2, rope


def _rope_tables(pos, rope):
    half = rope // 2
    inv_freq = ROPE_THETA ** (-jnp.arange(half, dtype=F32) / half)
    ang = pos.astype(F32)[:, None] * inv_freq[None, :]
    cos, sin = jnp.cos(ang), jnp.sin(ang)
    cos_t = _pad_lanes(jnp.concatenate([cos, cos], axis=1))
    sin_t = _pad_lanes(jnp.concatenate([-sin, sin], axis=1))
    return cos_t, sin_t


def _mla_layer(x, j, cos_t, sin_t, prm, cache_kv, cache_ks, page_table, dims):
    batch, seq, bd, ns = dims
    tp = batch * seq
    w, wuv, wo2, rope = prm
    h, kvr, _ = wuv.shape
    scale = float(w['wukt'].shape[1] + rope) ** -0.5
    qlat, qpe, ckv, kpe, ks = _mla_proj(x, cos_t, sin_t, w, bm=512)
    kv_row = jnp.concatenate([ckv, kpe[:, :rope]], axis=1)
    ks8 = ks[:, :h]
    c_bf = ckv[:tp].astype(BF)
    kpe_bf = kpe[:tp].astype(BF)
    kst = jnp.transpose(ks8[:tp])
    y_p = _flash_prompt(qlat, qpe, c_bf, kpe_bf, kst, x, wuv, wo2, batch=batch, seq=seq, tq=FLASH_TQ,
                        tk=FLASH_TK, hc=FLASH_HC, scale=scale)
    page = cache_kv.shape[3]
    qlat_s =jnp.transpose(qlat[:, tp:].reshape(h, bd, ns, kvr), (1, 0, 2, 3)).reshape(bd, h * ns, kvr)
    qpe_s = jnp.transpose(qpe[:, tp:].reshape(h, bd, ns, LANES), (1, 0, 2, 3)).reshape(bd, h * ns, LANES)
    newkv = jnp.pad(jnp.transpose(kv_row[tp:].reshape(bd, ns, kvr + rope), (0, 2, 1)),
                    ((0, 0), (0, 0), (0, page - ns)))
    newkst = jnp.pad(jnp.transpose(ks8[tp:].reshape(bd, ns, h), (0, 2, 1)), ((0, 0), (0, 0), (0, page - ns)))
    o_s = _decode_attend(page_table, qlat_s, qpe_s, cache_kv, j, cache_ks, newkv, newkst, pps=DECODE_PPS,
                         scale=scale)
    o_s = jnp.transpose(o_s.reshape(bd, h, ns, kvr), (0, 2, 1, 3)).reshape(bd * ns, h * kvr)
    y_s = _mla_out(o_s, x[tp:], wuv, wo2, bm=min(512, bd * ns))
    y = jnp.concatenate([y_p, y_s], axis=0)
    return y, kv_row, ks8


def _gmlp_layer(x, g_mix_l, w_in, g_v, w_s, b_s, w_out, dims):
    batch, seq, bd, ns = dims
    tp = batch * seq
    groups, chunk, _ = w_s.shape
    tri = jnp.tril(jnp.ones((chunk, chunk), bool))
    ws_p = jnp.where(tri[None], w_s, 0).astype(BF)
    bs_p = _pad_lanes(jnp.transpose(b_s))
    l = min(ns, chunk)
    tri_s = jnp.tril(jnp.ones((l, l), bool))
    ws_small = jnp.where(tri_s[None], w_s[:, :l, :l], 0)
    eye = jnp.eye(chunk // l, dtype=F32)
    ws_s = jnp.einsum('ab,gts->gatbs', eye, ws_small).reshape(groups, chunk, chunk).astype(BF)
    bs_s = _pad_lanes(jnp.tile(jnp.transpose(b_s[:, :l]), (chunk // l, 1)))
    args = (_row(g_mix_l), w_in.astype(BF), _row(g_v))
    wout = w_out.astype(BF)
    (y_p,) = _gmlp(x[:tp], *args, ws_p, bs_p, wout, bm=256, emit_v=False)
    y_s, v_s = _gmlp(x[tp:], *args, ws_s, bs_s, wout, bm=256, emit_v=True)
    return jnp.concatenate([y_p, y_s], axis=0), v_s


def _rglru_layer(x, g_mix_l, h0_s, buf_s, w_gate, w_x, conv_w, conv_b, w_a, b_a, w_i, b_i, lam, w_out, dims):
    batch, seq, bd, ns = dims
    tp = batch * seq
    nb, db, _ = w_a.shape
    ec = nb * db
    pair = 2
    gw = db * pair

    def blockdiag(wb):
        wb = wb.reshape(nb // pair, pair, db, db)
        eye = jnp.eye(pair, dtype=wb.dtype)
        return jnp.einsum('kpde,pq->kpdqe', wb, eye).reshape(nb // pair, gw, gw).astype(BF)

    w = {'wgate': w_gate.astype(BF), 'wx': w_x.astype(BF), 'cw': _pad_rows(conv_w), 'cb': _row(conv_b),
         'wa': blockdiag(w_a), 'ba': _row(b_a), 'wi': blockdiag(w_i), 'bi': _row(b_i),
         'sp': _row(jax.nn.softplus(-lam.astype(F32))), 'wout': w_out.astype(BF)}
    g = _row(g_mix_l)
    cw = conv_w.shape[0]
    zeros_h = jnp.zeros((SUBLANES, ec), F32)
    zeros_b = jnp.zeros((1, SUBLANES, ec), F32)
    y_p, hl_p, xl_p = _rglru(x[:tp], g, zeros_h, zeros_b, w, bt=256, batch=batch, seq=seq, segmented=False)
    buf8 = jnp.pad(buf_s.astype(F32), ((0, 0), (SUBLANES - (cw - 1), 0), (0, 0)))
    y_s, hl_s, xl_s = _rglru(x[tp:], g, h0_s.astype(F32), buf8, w, bt=256, batch=bd, seq=ns, segmented=True)
    h_p = hl_p[:, 0, :]
    conv_p = xl_p[:, SUBLANES - (cw - 1):, :]
    conv_s = xl_s.reshape(bd, ns, ec)[:, ns - (cw - 1):, :]
    return jnp.concatenate([y_p, y_s], axis=0), h_p, conv_p, hl_s, conv_s


def _pad_rows(v, rows=SUBLANES):
    return jnp.pad(v.astype(F32), ((0, rows - v.shape[0]), (0, 0)))


def _moe_layer(x, g_ffn_l, router, w1, w3, w2, layer, *, bm=512, bf=1792):
    t, d = x.shape
    ne = router.shape[1]
    f, info, cnt = _router(x, _row(g_ffn_l), router, bm=512)
    counts = cnt[0, :ne].astype(jnp.int32)
    tiles_per = (counts + bm - 1) // bm
    tile_start = jnp.cumsum(tiles_per) - tiles_per
    n_tiles = (t * TOP_K) // bm + ne
    r_pad = n_tiles * bm
    tile_ids = jnp.arange(n_tiles)
    n_used = jnp.sum(tiles_per)
    tile_expert = jnp.sum((tile_ids[:, None] >= tile_start[None, :]).astype(jnp.int32), axis=1) - 1
    last_e = jnp.max(jnp.where(tiles_per > 0, jnp.arange(ne), 0))
    tile_expert = jnp.where(tile_ids < n_used, tile_expert, last_e).astype(jnp.int32)
    base = (tile_start * bm).astype(jnp.int32)
    sel = info[:, :TOP_K].astype(jnp.int32)
    dest = jnp.take(base, sel) + info[:, TOP_K:2 * TOP_K].astype(jnp.int32)
    npc = d // SC_WORDS
    off = (jnp.arange(npc, dtype=jnp.int32) * r_pad)[:, None]
    idx_a = (off + dest[None, :, 0]).reshape(-1)
    idx_b = (off + dest[None, :, 1]).reshape(-1)
    xs = _sc_dispatch(f.reshape(npc * t, SC_WORDS), idx_a, idx_b, npc * r_pad).reshape(npc, r_pad, SC_WORDS)
    ys = _ffn_grouped(tile_expert, n_used.reshape(1).astype(jnp.int32), xs,
                      w1, w3, w2, layer, bm=bm, bf=bf)
    ys_p = ys.reshape(npc * r_pad, SC_WORDS)
    ya = _sc_gather(ys_p, idx_a).reshape(npc, t, SC_WORDS)
    yb = _sc_gather(ys_p, idx_b).reshape(npc, t, SC_WORDS)
    return _moe_combine(x, ya, yb, info, bm=512)


def kernel(x_prompt, x_sample, cache_mla_kv, cache_mla_kscale, state_rglru_h, state_rglru_conv, page_table, g_mix, g_ffn, mla_wq_a, mla_g_qa, mla_wq_b, mla_wkv_a, mla_g_kva, mla_w_uk, mla_w_uv, mla_wo, mla_g_qn, mla_g_qr, mla_g_kn, mla_g_kr, gm_w_in, gm_g_v, gm_w_s, gm_b_s, gm_w_out, rg_w_gate, rg_w_x, rg_conv_w, rg_conv_b, rg_w_a, rg_b_a, rg_w_i, rg_b_i, rg_lam, rg_w_out, ffd_w1, ffd_w3, ffd_w2, moe_router, moe_w1, moe_w3, moe_w2):
    batch, seq, d = x_prompt.shape
    bd, ns, _ = x_sample.shape
    dims = (batch, seq, bd, ns)
    tp = batch * seq
    depth = g_mix.shape[0]
    n_mixers = 3
    past_len = page_table.shape[1] * cache_mla_kv.shape[2]
    cache_kv_t = jnp.swapaxes(cache_mla_kv, 2, 3)
    cache_ks_t = jnp.swapaxes(cache_mla_kscale, 2, 3)
    moe_w1_b, moe_w3_b, moe_w2_b = moe_w1.astype(BF), moe_w3.astype(BF), moe_w2.astype(BF)

    x = jnp.concatenate([x_prompt.reshape(tp, d), x_sample.reshape(bd * ns, d)], axis=0)
    rope = mla_wq_b.shape[-1] - mla_w_uk.shape[-1]
    pos = jnp.concatenate([jnp.tile(jnp.arange(seq, dtype=jnp.int32), batch),
                           jnp.tile(past_len + jnp.arange(ns, dtype=jnp.int32), bd)])
    cos_t, sin_t = _rope_tables(pos, rope)

    kv_out, ks_out, v_out, hp_out, cp_out, hs_out, cs_out = [], [], [], [], [], [], []
    counts = [0, 0, 0]
    for layer in range(depth):
        kind = layer % n_mixers
        j = counts[kind]
        counts[kind] += 1
        if kind == 0:
            prm = _prep_mla(g_mix[layer], mla_wq_a[j], mla_g_qa[j], mla_wq_b[j], mla_wkv_a[j], mla_g_kva[j],
                            mla_w_uk[j], mla_w_uv[j], mla_wo[j], mla_g_qn[j], mla_g_qr[j], mla_g_kn[j],
                            mla_g_kr[j])
            x, kv_row, ks8 = _mla_layer(x, j, cos_t, sin_t, prm, cache_kv_t, cache_ks_t, page_table, dims)
            kv_out.append(kv_row)
            ks_out.append(ks8)
        elif kind == 1:
            x, v_s = _gmlp_layer(x, g_mix[layer], gm_w_in[j], gm_g_v[j], gm_w_s[j], gm_b_s[j], gm_w_out[j], dims)
            v_out.append(v_s.reshape(bd, ns, -1))
        else:
            x, h_p, c_p, h_s, c_s = _rglru_layer(
                x, g_mix[layer], state_rglru_h[j], state_rglru_conv[j], rg_w_gate[j], rg_w_x[j], rg_conv_w[j],
                rg_conv_b[j], rg_w_a[j], rg_b_a[j], rg_w_i[j], rg_b_i[j], rg_lam[j], rg_w_out[j], dims)
            hp_out.append(h_p)
            cp_out.append(c_p)
            hs_out.append(h_s)
            cs_out.append(c_s)
        i = layer // 2
        if layer % 2 == 0:
            x = _ffn_dense(x, _row(g_ffn[layer]), ffd_w1[i].astype(BF), ffd_w3[i].astype(BF),
                           ffd_w2[i].astype(BF), bm=512, bf=1408)
        else:
            x = _moe_layer(x, g_ffn[layer], moe_router[i], moe_w1_b, moe_w3_b, moe_w2_b, i)

    kv_all = jnp.stack(kv_out)
    ks_all = jnp.stack(ks_out)
    return (x[:tp].reshape(batch, seq, d), x[tp:].reshape(bd, ns, d),
            kv_all[:, :tp].reshape(len(kv_out), batch, seq, -1), ks_all[:, :tp].reshape(len(ks_out), batch, seq, -1),
            kv_all[:, tp:].reshape(len(kv_out), bd, ns, -1), ks_all[:, tp:].reshape(len(ks_out), bd, ns, -1),
            jnp.stack(v_out), jnp.stack(hp_out), jnp.stack(cp_out), jnp.stack(hs_out), jnp.stack(cs_out))
```
